```python
import jax, jax.numpy as jnp
from jax import lax
import numpy as np

D_MODEL = 1024
BATCH = 8
SEQ = 4096
DEPTH = 4

N_A_LAYERS = DEPTH // 2
N_B_LAYERS = DEPTH - N_A_LAYERS
PLE_DIM = 256
D_FF = 4 * D_MODEL
NORM_EPS = 1e-6

A_HEADS = 8
A_KEY_DIM = 128
A_VAL_DIM = 128
A_CONV = 4
A_CHUNK = 64
A_QK = A_HEADS * A_KEY_DIM
A_V = A_HEADS * A_VAL_DIM
A_CONV_CH = 2 * A_QK + A_V
A_IN_COLS = 2 * A_QK + 2 * A_V + 2 * A_HEADS

B_HEAD_DIM = 128
B_GROUPS = ((128, 1), (512, 4), (2048, 16))
B_N_GROUPS = len(B_GROUPS)
B_HEADS_PER_GROUP = 8
B_KV_PER_GROUP = 2
B_Q_COLS = B_N_GROUPS * B_HEADS_PER_GROUP * B_HEAD_DIM
B_KV_COLS = 2 * B_N_GROUPS * B_KV_PER_GROUP * B_HEAD_DIM
B_OUT = B_HEADS_PER_GROUP * B_HEAD_DIM

ROPE_THETA = 500000.0
ROPE_DIM = B_HEAD_DIM // 4

kernel_name = "yoco_gdn_dilated_swa_hybrid"

F32 = jnp.float32


def rms_norm(x, w):
    xf = x.astype(F32)
    y = xf * lax.rsqrt(jnp.mean(xf * xf, axis=-1, keepdims=True) + NORM_EPS)
    return (y * w.astype(F32)).astype(x.dtype)


def l2_norm(x):
    return x * lax.rsqrt(jnp.sum(x * x, axis=-1, keepdims=True) + NORM_EPS)


def rope_tables(seq):
    inv = jnp.power(jnp.float32(ROPE_THETA), -jnp.arange(0, ROPE_DIM, 2, dtype=F32) / ROPE_DIM)
    ang = jnp.arange(seq, dtype=F32)[:, None] * inv[None, :]
    return jnp.cos(ang), jnp.sin(ang)


def apply_partial_rope(x, cos, sin):
    half = ROPE_DIM // 2
    shape = (1, x.shape[1]) + (1,) * (x.ndim - 3) + (half,)
    c = cos.reshape(shape)
    s = sin.reshape(shape)
    xf = x.astype(F32)
    x1 = xf[..., :half]
    x2 = xf[..., half:ROPE_DIM]
    return jnp.concatenate([x1 * c - x2 * s, x2 * c + x1 * s, xf[..., ROPE_DIM:]], axis=-1).astype(x.dtype)


def causal_depthwise_conv(x, w):
    c = x.shape[-1]
    return lax.conv_general_dilated(x, w[:, None, :].astype(x.dtype), window_strides=(1,),
                                    padding=[(w.shape[0] - 1, 0)],
                                    dimension_numbers=('NWC', 'WIO', 'NWC'),
                                    feature_group_count=c)


def chunk_gated_delta_rule(q, k, v, g, beta):
    bsz, seq, heads, dk = q.shape
    dv = v.shape[-1]
    c = A_CHUNK
    n = seq // c

    def chunks(t):
        return jnp.moveaxis(t.reshape((bsz, n, c, heads) + t.shape[3:]), 3, 1)

    q = chunks(q) * (dk ** -0.5)
    k = chunks(k)
    v = chunks(v)
    g = chunks(g)
    beta = chunks(beta)
    gc = jnp.cumsum(g, axis=-1)
    idx = jnp.arange(c)
    incl = idx[:, None] >= idx[None, :]
    strict = idx[:, None] > idx[None, :]
    decay = jnp.exp(jnp.where(incl, gc[..., :, None] - gc[..., None, :], -jnp.inf))
    kb = k * beta[..., None]
    lower = jnp.where(strict, jnp.einsum('bhncd,bhnjd->bhncj', kb, k) * decay, 0.0)
    eye = jnp.eye(c, dtype=F32)
    tmat = lax.linalg.triangular_solve(lower + eye, jnp.broadcast_to(eye, lower.shape),
                                       left_side=True, lower=True, unit_diagonal=True)
    u = tmat @ (v * beta[..., None])
    w = tmat @ (kb * jnp.exp(gc)[..., None])
    attn = jnp.where(incl, jnp.einsum('bhncd,bhnjd->bhncj', q, k) * decay, 0.0)
    q_dec = q * jnp.exp(gc)[..., None]
    k_dec = k * jnp.exp(gc[..., -1:] - gc)[..., None]
    chunk_dec = jnp.exp(gc[..., -1])

    def step(state, xs):
        u_n, w_n, attn_n, qd_n, kd_n, cd_n = xs
        v_new = u_n - jnp.einsum('bhcd,bhde->bhce', w_n, state)
        o_n = jnp.einsum('bhcd,bhde->bhce', qd_n, state) + jnp.einsum('bhcj,bhje->bhce', attn_n, v_new)
        state = state * cd_n[..., None, None] + jnp.einsum('bhcd,bhce->bhde', kd_n, v_new)
        return state, o_n

    xs = (jnp.moveaxis(u, 2, 0), jnp.moveaxis(w, 2, 0), jnp.moveaxis(attn, 2, 0),
          jnp.moveaxis(q_dec, 2, 0), jnp.moveaxis(k_dec, 2, 0), jnp.moveaxis(chunk_dec, 2, 0))
    state0 = jnp.zeros((bsz, heads, dk, dv), F32)
    _, o = lax.scan(step, state0, xs)
    return jnp.transpose(o, (1, 0, 3, 2, 4)).reshape(bsz, seq, heads, dv)


def gated_deltanet(h, w_in, conv_w, a_log, dt_bias, out_norm, w_out):
    bsz, seq, _ = h.shape
    proj = h @ w_in
    qkv = jax.nn.silu(causal_depthwise_conv(proj[..., :A_CONV_CH], conv_w)).astype(F32)
    q = l2_norm(qkv[..., :A_QK].reshape(bsz, seq, A_HEADS, A_KEY_DIM))
    k = l2_norm(qkv[..., A_QK:2 * A_QK].reshape(bsz, seq, A_HEADS, A_KEY_DIM))
    v = qkv[..., 2 * A_QK:].reshape(bsz, seq, A_HEADS, A_VAL_DIM)
    z = proj[..., A_CONV_CH:A_CONV_CH + A_V].astype(F32).reshape(bsz, seq, A_HEADS, A_VAL_DIM)
    b = proj[..., A_CONV_CH + A_V:A_CONV_CH + A_V + A_HEADS].astype(F32)
    a = proj[..., A_CONV_CH + A_V + A_HEADS:].astype(F32)
    beta = jax.nn.sigmoid(b)
    g = -jnp.exp(a_log.astype(F32)) * jax.nn.softplus(a + dt_bias.astype(F32))
    o = chunk_gated_delta_rule(q, k, v, g, beta)
    o = o * lax.rsqrt(jnp.mean(o * o, axis=-1, keepdims=True) + NORM_EPS) * out_norm.astype(F32) * jax.nn.silu(z)
    return o.reshape(bsz, seq, A_V).astype(h.dtype) @ w_out


def dilated_group_attention(q, k, v, window, dilation):
    bsz, seq, hq, dh = q.shape
    hkv = k.shape[2]
    rep = hq // hkv
    blk = window // dilation
    length = seq // dilation
    nb = -(-length // blk)
    lp = nb * blk
    z = bsz * dilation

    def to_blocks(t):
        hh = t.shape[2]
        t = t.reshape(bsz, length, dilation, hh, dh).transpose(0, 2, 1, 3, 4).reshape(z, length, hh, dh)
        t = jnp.pad(t, ((0, 0), (0, lp - length), (0, 0), (0, 0)))
        return t.reshape(z, nb, blk, hh, dh)

    def with_prev(t):
        prev = jnp.pad(t, ((0, 0), (1, 0), (0, 0), (0, 0), (0, 0)))[:, :-1]
        return jnp.concatenate([prev, t], axis=2)

    qb = to_blocks(q).reshape(z, nb, blk, hkv, rep, dh)
    kk = with_prev(to_blocks(k))
    vv = with_prev(to_blocks(v))
    s = jnp.einsum('znqgrd,znkgd->zngrqk', qb, kk, preferred_element_type=F32) * (dh ** -0.5)
    qi = jnp.arange(blk)[:, None]
    kj = jnp.arange(2 * blk)[None, :]
    dist = qi + blk - kj
    band = (dist >= 0) & (dist <= blk)
    mask = band[None] & ((jnp.arange(nb)[:, None, None] > 0) | (kj >= blk)[None])
    s = jnp.where(mask[None, :, None, None], s, -jnp.inf)
    m = jnp.max(s, axis=-1, keepdims=True)
    e = jnp.exp(s - m)
    den = jnp.sum(e, axis=-1)
    o = jnp.einsum('zngrqk,znkgd->znqgrd', e, vv.astype(F32)) / jnp.transpose(den, (0, 1, 4, 2, 3))[..., None]
    lse = jnp.transpose(m[..., 0] + jnp.log(den), (0, 1, 4, 2, 3))
    o = o.reshape(z, lp, hq, dh)[:, :length]
    lse = lse.reshape(z, lp, hq)[:, :length]
    o = o.reshape(bsz, dilation, length, hq, dh).transpose(0, 2, 1, 3, 4).reshape(bsz, seq, hq, dh)
    lse = lse.reshape(bsz, dilation, length, hq).transpose(0, 2, 1, 3).reshape(bsz, seq, hq)
    return o, lse


def dilated_mixture_attention(h, k_shared, v_shared, w_q, w_o, cos, sin):
    bsz, seq, _ = h.shape
    q = (h @ w_q).reshape(bsz, seq, B_N_GROUPS, B_HEADS_PER_GROUP, B_HEAD_DIM)
    q = apply_partial_rope(q, cos, sin)
    outs = []
    lses = []
    for gi, (window, dilation) in enumerate(B_GROUPS):
        o_g, lse_g = dilated_group_attention(q[:, :, gi], k_shared[:, :, gi], v_shared[:, :, gi], window, dilation)
        outs.append(o_g)
        lses.append(lse_g)
    wts = jax.nn.softmax(jnp.stack(lses, axis=0), axis=0)
    o = jnp.sum(wts[..., None] * jnp.stack(outs, axis=0), axis=0)
    return o.reshape(bsz, seq, B_OUT).astype(h.dtype) @ w_o


def squared_relu_mlp(h, w_up, w_down):
    return jnp.square(jax.nn.relu(h @ w_up)) @ w_down


def setup_inputs(seed: int = 0) -> dict:
    key = jax.random.key(seed)
    ks = jax.random.split(key, 20)

    def dense(k, shape, fan_in):
        return jax.random.normal(k, shape, F32) * (fan_in ** -0.5)

    def gain(k, shape):
        return 1.0 + 0.02 * jax.random.normal(k, shape, F32)

    return {
        'x': jax.random.normal(ks[0], (BATCH, SEQ, D_MODEL), F32),
        'p': jax.random.normal(ks[1], (DEPTH, BATCH, SEQ, PLE_DIM), F32),
        'attn_norm': gain(ks[2], (DEPTH, D_MODEL)),
        'mlp_norm': gain(ks[3], (DEPTH, D_MODEL)),
        'a_w_in': dense(ks[4], (N_A_LAYERS, D_MODEL, A_IN_COLS), D_MODEL),
        'a_conv_w': dense(ks[5], (N_A_LAYERS, A_CONV, A_CONV_CH), A_CONV),
        'a_log': jnp.log(jax.random.uniform(ks[6], (N_A_LAYERS, A_HEADS), F32, 1.0, 16.0)),
        'a_dt_bias': 0.1 * jax.random.normal(ks[7], (N_A_LAYERS, A_HEADS), F32),
        'a_out_norm': gain(ks[8], (N_A_LAYERS, A_VAL_DIM)),
        'a_w_out': dense(ks[9], (N_A_LAYERS, A_V, D_MODEL), A_V),
        'kv_norm': gain(ks[10], (D_MODEL,)),
        'b_w_kv': dense(ks[11], (D_MODEL, B_KV_COLS), D_MODEL),
        'b_w_q': dense(ks[12], (N_B_LAYERS, D_MODEL, B_Q_COLS), D_MODEL),
        'b_w_o': dense(ks[13], (N_B_LAYERS, B_OUT, D_MODEL), B_OUT),
        'mlp_w_up': dense(ks[14], (DEPTH, D_MODEL, D_FF), D_MODEL),
        'mlp_w_down': dense(ks[15], (DEPTH, D_FF, D_MODEL), D_FF),
        'ple_w_proj': dense(ks[16], (DEPTH, PLE_DIM, D_MODEL), PLE_DIM),
        'ple_w_gate': dense(ks[17], (DEPTH, D_MODEL, D_MODEL), D_MODEL),
        'final_norm': gain(ks[18], (D_MODEL,)),
    }


def reference(x, p, attn_norm, mlp_norm, a_w_in, a_conv_w, a_log, a_dt_bias, a_out_norm, a_w_out,
              kv_norm, b_w_kv, b_w_q, b_w_o, mlp_w_up, mlp_w_down, ple_w_proj, ple_w_gate, final_norm):
    bsz, seq, _ = x.shape
    cos, sin = rope_tables(seq)
    k_shared = None
    v_shared = None
    for i in range(DEPTH):
        if i < N_A_LAYERS:
            h = rms_norm(x, attn_norm[i])
            x = x + gated_deltanet(h, a_w_in[i], a_conv_w[i], a_log[i], a_dt_bias[i], a_out_norm[i], a_w_out[i])
        else:
            if i == N_A_LAYERS:
                kv = (rms_norm(x, kv_norm) @ b_w_kv).reshape(bsz, seq, 2, B_N_GROUPS, B_KV_PER_GROUP, B_HEAD_DIM)
                k_shared = apply_partial_rope(kv[:, :, 0], cos, sin)
                v_shared = kv[:, :, 1]
            j = i - N_A_LAYERS
            h = rms_norm(x, attn_norm[i])
            x = x + dilated_mixture_attention(h, k_shared, v_shared, b_w_q[j], b_w_o[j], cos, sin)
        h = rms_norm(x, mlp_norm[i])
        x = x + squared_relu_mlp(h, mlp_w_up[i], mlp_w_down[i])
        x = x + jax.nn.sigmoid(x @ ple_w_gate[i]) * (p[i].astype(x.dtype) @ ple_w_proj[i])
    return rms_norm(x, final_norm)
```

```python
import functools

import jax
import jax.numpy as jnp
from jax import lax
from jax.experimental import pallas as pl
from jax.experimental.pallas import tpu as pltpu

F32 = jnp.float32
BF16 = jnp.bfloat16

D_MODEL = 1024
DEPTH = 4
N_A_LAYERS = DEPTH // 2
PLE_DIM = 256
D_FF = 4 * D_MODEL
NORM_EPS = 1e-6

A_HEADS = 8
A_DIM = 128
A_CONV = 4
A_CHUNK = 64
A_QK = A_HEADS * A_DIM
A_IN_COLS = 4 * A_QK + 2 * A_HEADS
LANES = 128
A_IN_PAD = 4 * A_QK + LANES

B_HEAD_DIM = 128
B_GROUPS = ((128, 1), (512, 4), (2048, 16))
B_N_GROUPS = len(B_GROUPS)
B_HEADS = 8
B_KV_HEADS = 2
B_REP = B_HEADS // B_KV_HEADS
B_BLK = 128
B_Q_COLS = B_N_GROUPS * B_HEADS * B_HEAD_DIM
B_K_COLS = B_N_GROUPS * B_KV_HEADS * B_HEAD_DIM
B_OUT = B_HEADS * B_HEAD_DIM
ROPE_THETA = 500000.0
ROPE_DIM = B_HEAD_DIM // 4
ROPE_HALF = ROPE_DIM // 2

VMEM_LIMIT = 56 * 1024 * 1024

PROJ_TM = 512
GDN_TS = 256
TAIL_TM = 512
TAIL_TF = 1024


def _dot(a, b):
    return jnp.dot(a, b, preferred_element_type=F32)


def _dot_nt(a, b):
    return lax.dot_general(a, b, (((1,), (1,)), ((), ())), preferred_element_type=F32)


def _rms(x, gain):
    return x * lax.rsqrt(jnp.mean(x * x, axis=-1, keepdims=True) + NORM_EPS) * gain


def _sigmoid(x):
    return 1.0 / (1.0 + jnp.exp(-x))


def _rope(y, c, s_lo, s_hi):
    return (y * c + pltpu.roll(y, LANES - ROPE_HALF, axis=1) * s_lo
            + pltpu.roll(y, ROPE_HALF, axis=1) * s_hi)


def _proj_kernel(x_ref, g_ref, w_ref, *rest, rope_tiles):
    if rope_tiles:
        c_ref, slo_ref, shi_ref, o_ref, h_ref = rest
    else:
        o_ref, h_ref = rest
    j = pl.program_id(1)

    @pl.when(j == 0)
    def _():
        h_ref[...] = _rms(x_ref[...], g_ref[...]).astype(BF16)

    y = _dot(h_ref[...], w_ref[...])
    if not rope_tiles:
        o_ref[...] = y.astype(o_ref.dtype)
        return

    @pl.when(j < rope_tiles)
    def _():
        c, s_lo, s_hi = c_ref[...], slo_ref[...], shi_ref[...]
        for b in range(y.shape[1] // LANES):
            sl = slice(b * LANES, (b + 1) * LANES)
            o_ref[:, sl] = _rope(y[:, sl], c, s_lo, s_hi).astype(o_ref.dtype)

    @pl.when(j >= rope_tiles)
    def _():
        o_ref[...] = y.astype(o_ref.dtype)


def _norm_proj(x, gain, w, *, tn, out_dtype, rope=None, rope_cols=0, seq=None):
    m, d = x.shape
    n = w.shape[1]
    tm = PROJ_TM
    in_specs = [pl.BlockSpec((tm, d), lambda i, j: (i, 0)),
                pl.BlockSpec((1, d), lambda i, j: (0, 0)),
                pl.BlockSpec((d, tn), lambda i, j: (0, j))]
    args = [x, gain.reshape(1, d), w]
    if rope_cols:
        sblk = seq // tm
        tab = pl.BlockSpec((tm, LANES), lambda i, j: (i % sblk, 0))
        in_specs += [tab, tab, tab]
        args += list(rope)
    return pl.pallas_call(
        functools.partial(_proj_kernel, rope_tiles=rope_cols // tn),
        grid=(m // tm, n // tn),
        in_specs=in_specs,
        out_specs=pl.BlockSpec((tm, tn), lambda i, j: (i, j)),
        out_shape=jax.ShapeDtypeStruct((m, n), out_dtype),
        scratch_shapes=[pltpu.VMEM((tm, d), BF16)],
        compiler_params=pltpu.CompilerParams(
            dimension_semantics=("parallel", "arbitrary"), vmem_limit_bytes=VMEM_LIMIT),
        name="norm_proj",
    )(*args)


def _conv_silu(x, prev8, w):
    row8 = lax.broadcasted_iota(jnp.int32, (8, LANES), 0)
    acc = x * w[A_CONV - 1:A_CONV, :]
    for s in range(1, A_CONV):
        rolled = pltpu.roll(x, s, axis=0)
        head = jnp.where(row8 < s, pltpu.roll(prev8, s, axis=0), rolled[0:8])
        shifted = jnp.concatenate([head, rolled[8:]], axis=0)
        acc = acc + shifted * w[A_CONV - 1 - s:A_CONV - s, :]
    return acc * _sigmoid(acc)


def _l2(x):
    return x * lax.rsqrt(jnp.sum(x * x, axis=-1, keepdims=True) + NORM_EPS)


def _unit_lower_inverse(low):
    c = low.shape[0]
    row = lax.broadcasted_iota(jnp.int32, (c, c), 0)
    col = lax.broadcasted_iota(jnp.int32, (c, c), 1)
    inv = jnp.where(row == col, 1.0, 0.0) - low
    power = low.astype(BF16)
    k = 2
    while k < c:
        power_f = _dot(power, power)
        power = power_f.astype(BF16)
        inv = inv + _dot(inv.astype(BF16), power)
        k *= 2
    return inv


def _gdn_kernel(q_ref, k_ref, v_ref, z_ref, gate_ref, cwq_ref, cwk_ref, cwv_ref,
                alog_ref, dt_ref, onorm_ref, o_ref, state_ref, carry_ref, gt_ref):
    head = pl.program_id(1)
    sb = pl.program_id(2)
    ts = q_ref.shape[1]
    c = A_CHUNK

    @pl.when(sb == 0)
    def _():
        state_ref[...] = jnp.zeros_like(state_ref)
        carry_ref[...] = jnp.zeros_like(carry_ref)

    raw = (q_ref[0], k_ref[0], v_ref[0])
    q = _l2(_conv_silu(raw[0], carry_ref[0], cwq_ref[...])) * (A_DIM ** -0.5)
    k = _l2(_conv_silu(raw[1], carry_ref[1], cwk_ref[...]))
    v = _conv_silu(raw[2], carry_ref[2], cwv_ref[...])
    for t in range(3):
        carry_ref[t] = raw[t][ts - 8:ts, :]

    gates = gate_ref[0]
    lane = lax.broadcasted_iota(jnp.int32, (ts, LANES), 1)
    a = gates + dt_ref[...]
    softplus = jnp.maximum(a, 0.0) + jnp.log(1.0 + jnp.exp(-jnp.abs(a)))
    g = -jnp.exp(alog_ref[...]) * softplus
    row_in_chunk = lax.broadcasted_iota(jnp.int32, (ts, LANES), 0) & (c - 1)
    s = 1
    while s < c:
        g = g + jnp.where(row_in_chunk >= s, pltpu.roll(g, s, axis=0), 0.0)
        s *= 2
    gt_ref[...] = g.T
    g_row = gt_ref[pl.ds(A_HEADS + head, 1), :]
    g_col = jnp.sum(jnp.where(lane == A_HEADS + head, g, 0.0), axis=-1, keepdims=True)
    beta = jnp.sum(jnp.where(lane == head, _sigmoid(gates), 0.0), axis=-1, keepdims=True)

    ri = lax.broadcasted_iota(jnp.int32, (c, c), 0)
    ci = lax.broadcasted_iota(jnp.int32, (c, c), 1)
    incl = ri >= ci
    strict = ri > ci
    onorm = onorm_ref[...]
    state = state_ref[...]
    for n in range(ts // c):
        sl = slice(n * c, (n + 1) * c)
        qc, kc, vc, bc, gc = q[sl], k[sl], v[sl], beta[sl], g_col[sl]
        kb = kc * bc
        qk = _dot_nt(jnp.concatenate([kb, qc], axis=0).astype(BF16), kc.astype(BF16))
        decay = jnp.exp(jnp.where(incl, gc - g_row[:, sl], -jnp.inf))
        lower = jnp.where(strict, qk[:c] * decay, 0.0)
        attn = jnp.where(incl, qk[c:] * decay, 0.0)
        tmat = _unit_lower_inverse(lower)
        egc = jnp.exp(gc)
        rhs = jnp.concatenate([vc * bc, kb * egc], axis=1).astype(BF16)
        uw = _dot(tmat.astype(BF16), rhs)
        u, w = uw[:, :A_DIM], uw[:, A_DIM:]
        ws = _dot(jnp.concatenate([w, qc * egc], axis=0).astype(BF16), state.astype(BF16))
        v_new = u - ws[:c]
        v_new_b = v_new.astype(BF16)
        o = ws[c:] + _dot(attn.astype(BF16), v_new_b)
        g_last = gc[c - 1:c, :]
        k_dec = kc * jnp.exp(g_last - gc)
        state = state * jnp.exp(g_last) + _dot(k_dec.T.astype(BF16), v_new_b)
        zc = z_ref[0, sl, :]
        o = o * lax.rsqrt(jnp.mean(o * o, axis=-1, keepdims=True) + NORM_EPS)
        o_ref[0, sl, :] = (o * onorm * (zc * _sigmoid(zc))).astype(o_ref.dtype)
    state_ref[...] = state


def _gated_deltanet(proj, conv_w, a_log, dt_bias, out_norm):
    bsz, seq, _ = proj.shape
    ts = GDN_TS
    nh = A_HEADS

    def col(off):
        return pl.BlockSpec((1, ts, LANES), lambda b, h, s: (b, s, off + h))

    def cw(off):
        return pl.BlockSpec((A_CONV, LANES), lambda b, h, s: (0, off + h))

    row = pl.BlockSpec((1, LANES), lambda b, h, s: (0, 0))
    pad = jnp.zeros((LANES - 2 * nh,), F32)
    alog_row = jnp.concatenate([jnp.zeros((nh,), F32), a_log, pad]).reshape(1, LANES)
    dt_row = jnp.concatenate([jnp.zeros((nh,), F32), dt_bias, pad]).reshape(1, LANES)
    return pl.pallas_call(
        _gdn_kernel,
        grid=(bsz, nh, seq // ts),
        in_specs=[col(0), col(nh), col(2 * nh), col(3 * nh),
                  pl.BlockSpec((1, ts, LANES), lambda b, h, s: (b, s, 4 * nh)),
                  cw(0), cw(nh), cw(2 * nh), row, row, row],
        out_specs=pl.BlockSpec((1, ts, LANES), lambda b, h, s: (b, s, h)),
        out_shape=jax.ShapeDtypeStruct((bsz, seq, nh * A_DIM), BF16),
        scratch_shapes=[pltpu.VMEM((A_DIM, A_DIM), F32),
                        pltpu.VMEM((3, 8, LANES), F32),
                        pltpu.VMEM((LANES, ts), F32)],
        compiler_params=pltpu.CompilerParams(
            dimension_semantics=("parallel", "parallel", "arbitrary"),
            vmem_limit_bytes=VMEM_LIMIT),
        name="gated_delta_rule",
    )(proj, proj, proj, proj, proj, conv_w, conv_w, conv_w, alog_row, dt_row,
      out_norm.reshape(1, LANES))


def _attn_kernel(q_ref, kc_ref, kp_ref, vc_ref, vp_ref, o_ref, lse_ref):
    nblk = pl.program_id(2)
    blk = B_BLK
    dh = B_HEAD_DIM
    rows = B_REP * blk
    kk_all = jnp.concatenate([kp_ref[0], kc_ref[0]], axis=0)
    vv_all = jnp.concatenate([vp_ref[0], vc_ref[0]], axis=0)
    qi = lax.broadcasted_iota(jnp.int32, (rows, 2 * blk), 0) & (blk - 1)
    kj = lax.broadcasted_iota(jnp.int32, (rows, 2 * blk), 1)
    dist = qi + blk - kj
    mask = (dist >= 0) & (dist <= blk) & ((kj >= blk) | (nblk > 0))
    lane = lax.broadcasted_iota(jnp.int32, (blk, LANES), 1)
    lse_tile = jnp.zeros((blk, LANES), F32)
    for g in range(B_KV_HEADS):
        qs = jnp.concatenate(
            [q_ref[0, :, (g * B_REP + r) * dh:(g * B_REP + r + 1) * dh] for r in range(B_REP)],
            axis=0)
        s = _dot_nt(qs, kk_all[:, g * dh:(g + 1) * dh]) * (dh ** -0.5)
        s = jnp.where(mask, s, -jnp.inf)
        m = jnp.max(s, axis=-1, keepdims=True)
        e = jnp.exp(s - m)
        den = jnp.sum(e, axis=-1, keepdims=True)
        o = _dot(e.astype(BF16), vv_all[:, g * dh:(g + 1) * dh]) / den
        lse = m + jnp.log(den)
        for r in range(B_REP):
            hd = g * B_REP + r
            o_ref[0, :, hd * dh:(hd + 1) * dh] = o[r * blk:(r + 1) * blk]
            lse_tile = jnp.where(lane == hd, lse[r * blk:(r + 1) * blk], lse_tile)
    lse_ref[0] = lse_tile


def _dilated_attention(q, kv, gi, dilation):
    bsz, seq, _ = q.shape
    length = seq // dilation
    nb = length // B_BLK
    kvw = B_KV_HEADS * B_HEAD_DIM
    qv = q.reshape(bsz, length, dilation * B_Q_COLS)
    kvv = kv.reshape(bsz, length, dilation * 2 * B_K_COLS)
    nq = B_N_GROUPS
    nk = 2 * B_N_GROUPS

    def kv_spec(off, prev):
        if prev:
            return pl.BlockSpec((1, B_BLK, kvw),
                                lambda b, r, n: (b, jnp.maximum(n - 1, 0), r * nk + off))
        return pl.BlockSpec((1, B_BLK, kvw), lambda b, r, n: (b, n, r * nk + off))

    o, lse = pl.pallas_call(
        _attn_kernel,
        grid=(bsz, dilation, nb),
        in_specs=[pl.BlockSpec((1, B_BLK, B_OUT), lambda b, r, n: (b, n, r * nq + gi)),
                  kv_spec(gi, False), kv_spec(gi, True),
                  kv_spec(B_N_GROUPS + gi, False), kv_spec(B_N_GROUPS + gi, True)],
        out_specs=[pl.BlockSpec((1, B_BLK, B_OUT), lambda b, r, n: (b, n, r)),
                   pl.BlockSpec((1, B_BLK, LANES), lambda b, r, n: (b, n, r))],
        out_shape=[jax.ShapeDtypeStruct((bsz, length, dilation * B_OUT), F32),
                   jax.ShapeDtypeStruct((bsz, length, dilation * LANES), F32)],
        compiler_params=pltpu.CompilerParams(
            dimension_semantics=("parallel", "parallel", "arbitrary"),
            vmem_limit_bytes=VMEM_LIMIT),
        name=f"dilated_attention_g{gi}",
    )(qv, kvv, kvv, kvv, kvv)
    return o.reshape(bsz * seq, B_OUT), lse.reshape(bsz * seq, LANES)


def _tail_kernel(*refs, n_mix, final):
    x_ref = refs[0]
    o_refs = refs[1:1 + n_mix]
    n_lse = n_mix if n_mix > 1 else 0
    lse_refs = refs[1 + n_mix:1 + n_mix + n_lse]
    rest = refs[1 + n_mix + n_lse:]
    (p_ref, wo_ref, gm_ref, wup_ref, wdn_ref, wg_ref, wp_ref, gf_ref,
     out_ref, x1_ref, h_ref, acc_ref) = rest
    f = pl.program_id(1)

    @pl.when(f == 0)
    def _():
        if n_mix > 1:
            lses = [r[...] for r in lse_refs]
            mx = functools.reduce(jnp.maximum, lses)
            wts = [jnp.exp(l - mx) for l in lses]
            inv = 1.0 / functools.reduce(jnp.add, wts)
            wts = [w * inv for w in wts]
            pieces = []
            for hd in range(B_HEADS):
                sl = slice(hd * B_HEAD_DIM, (hd + 1) * B_HEAD_DIM)
                acc = wts[0][:, hd:hd + 1] * o_refs[0][:, sl]
                for g in range(1, n_mix):
                    acc = acc + wts[g][:, hd:hd + 1] * o_refs[g][:, sl]
                pieces.append(acc.astype(BF16))
            mixed = jnp.concatenate(pieces, axis=1)
        else:
            mixed = o_refs[0][...]
        x1 = x_ref[...] + _dot(mixed, wo_ref[...])
        x1_ref[...] = x1
        h_ref[...] = _rms(x1, gm_ref[...]).astype(BF16)
        acc_ref[...] = jnp.zeros_like(acc_ref)

    u = jnp.maximum(_dot(h_ref[...], wup_ref[...]), 0.0)
    acc_ref[...] += _dot((u * u).astype(BF16), wdn_ref[...])

    @pl.when(f == pl.num_programs(1) - 1)
    def _():
        x2 = x1_ref[...] + acc_ref[...]
        gate = _sigmoid(_dot(x2.astype(BF16), wg_ref[...]))
        x3 = x2 + gate * _dot(p_ref[...].astype(BF16), wp_ref[...])
        if final:
            x3 = _rms(x3, gf_ref[...])
        out_ref[...] = x3


def _layer_tail(x, mix, lses, p, w_o, g_mlp, w_up, w_dn, w_gate, w_ple, g_final, *, final):
    m, d = x.shape
    tm, tf = TAIL_TM, TAIL_TF
    n_mix = len(mix)

    def rows(width):
        return pl.BlockSpec((tm, width), lambda i, f: (i, 0))

    def whole(a):
        return pl.BlockSpec(a.shape, lambda i, f: (0, 0))

    g_mlp = g_mlp.reshape(1, d)
    g_final = g_final.reshape(1, d)
    in_specs = ([rows(d)] + [rows(d)] * n_mix + [rows(LANES)] * len(lses)
                + [rows(PLE_DIM), whole(w_o), whole(g_mlp),
                   pl.BlockSpec((d, tf), lambda i, f: (0, f)),
                   pl.BlockSpec((tf, d), lambda i, f: (f, 0)),
                   whole(w_gate), whole(w_ple), whole(g_final)])
    return pl.pallas_call(
        functools.partial(_tail_kernel, n_mix=n_mix, final=final),
        grid=(m // tm, D_FF // tf),
        in_specs=in_specs,
        out_specs=rows(d),
        out_shape=jax.ShapeDtypeStruct((m, d), F32),
        scratch_shapes=[pltpu.VMEM((tm, d), F32), pltpu.VMEM((tm, d), BF16),
                        pltpu.VMEM((tm, d), F32)],
        compiler_params=pltpu.CompilerParams(
            dimension_semantics=("parallel", "arbitrary"), vmem_limit_bytes=VMEM_LIMIT),
        name="layer_tail",
    )(x, *mix, *lses, p, w_o, g_mlp, w_up, w_dn, w_gate, w_ple, g_final)


def _rope_tables(seq):
    inv = jnp.power(jnp.float32(ROPE_THETA), -jnp.arange(0, ROPE_DIM, 2, dtype=F32) / ROPE_DIM)
    ang = jnp.arange(seq, dtype=F32)[:, None] * inv[None, :]
    cos, sin = jnp.cos(ang), jnp.sin(ang)
    zeros = jnp.zeros((seq, LANES - ROPE_DIM), F32)
    zhalf = jnp.zeros((seq, ROPE_HALF), F32)
    c = jnp.concatenate([cos, cos, jnp.ones_like(zeros)], axis=1)
    s_lo = jnp.concatenate([-sin, zhalf, zeros], axis=1)
    s_hi = jnp.concatenate([zhalf, sin, zeros], axis=1)
    return c, s_lo, s_hi


def kernel(x, p, attn_norm, mlp_norm, a_w_in, a_conv_w, a_log, a_dt_bias, a_out_norm, a_w_out,
           kv_norm, b_w_kv, b_w_q, b_w_o, mlp_w_up, mlp_w_down, ple_w_proj, ple_w_gate, final_norm):
    bsz, seq, d = x.shape
    m = bsz * seq
    xf = x.reshape(m, d)
    rope = _rope_tables(seq)
    kv = None
    for i in range(DEPTH):
        if i < N_A_LAYERS:
            w_in = jnp.pad(a_w_in[i], ((0, 0), (0, A_IN_PAD - A_IN_COLS))).astype(BF16)
            proj = _norm_proj(xf, attn_norm[i], w_in, tn=A_IN_PAD // 3, out_dtype=F32)
            o = _gated_deltanet(proj.reshape(bsz, seq, A_IN_PAD), a_conv_w[i], a_log[i],
                                a_dt_bias[i], a_out_norm[i])
            mix, lses, w_o = [o.reshape(m, A_QK)], [], a_w_out[i]
        else:
            j = i - N_A_LAYERS
            if kv is None:
                kv = _norm_proj(xf, kv_norm, b_w_kv.astype(BF16), tn=B_K_COLS, out_dtype=BF16,
                                rope=rope, rope_cols=B_K_COLS, seq=seq).reshape(bsz, seq, -1)
            q = _norm_proj(xf, attn_norm[i], b_w_q[j].astype(BF16), tn=B_OUT, out_dtype=BF16,
                           rope=rope, rope_cols=B_Q_COLS, seq=seq).reshape(bsz, seq, -1)
            mix, lses = [], []
            for gi, (_, dilation) in enumerate(B_GROUPS):
                o_g, lse_g = _dilated_attention(q, kv, gi, dilation)
                mix.append(o_g)
                lses.append(lse_g)
            w_o = b_w_o[j]
        xf = _layer_tail(xf, mix, lses, p[i].reshape(m, PLE_DIM), w_o.astype(BF16), mlp_norm[i],
                         mlp_w_up[i].astype(BF16), mlp_w_down[i].astype(BF16),
                         ple_w_gate[i].astype(BF16), ple_w_proj[i].astype(BF16), final_norm,
                         final=(i == DEPTH - 1))
    return xf.reshape(bsz, seq, d)
```

```python
import functools

import jax
import jax.numpy as jnp
from jax import lax
from jax.experimental import pallas as pl
from jax.experimental.pallas import tpu as pltpu

F32 = jnp.float32
BF16 = jnp.bfloat16

D_MODEL = 1024
DEPTH = 4
N_A_LAYERS = DEPTH // 2
PLE_DIM = 256
D_FF = 4 * D_MODEL
NORM_EPS = 1e-6

A_HEADS = 8
A_DIM = 128
A_CONV = 4
A_CHUNK = 64
A_QK = A_HEADS * A_DIM
A_IN_COLS = 4 * A_QK + 2 * A_HEADS
LANES = 128
A_IN_PAD = 4 * A_QK + LANES

B_HEAD_DIM = 128
B_GROUPS = ((128, 1), (512, 4), (2048, 16))
B_N_GROUPS = len(B_GROUPS)
B_HEADS = 8
B_KV_HEADS = 2
B_REP = B_HEADS // B_KV_HEADS
B_BLK = 128
B_OUT = B_HEADS * B_HEAD_DIM
B_KW = B_KV_HEADS * B_HEAD_DIM
ROPE_THETA = 500000.0
ROPE_DIM = B_HEAD_DIM // 4
ROPE_HALF = ROPE_DIM // 2

VMEM_LIMIT = 56 * 1024 * 1024

PROJ_TM = 512
GDN_TS = 256
GDN_GROUP = 4
TAIL_TM = 512
TAIL_TF = 1024


def _dot(a, b):
    return jnp.dot(a, b, preferred_element_type=F32)


def _dot_nt(a, b):
    return lax.dot_general(a, b, (((1,), (1,)), ((), ())), preferred_element_type=F32)


def _rms(x, gain):
    return x * lax.rsqrt(jnp.mean(x * x, axis=-1, keepdims=True) + NORM_EPS) * gain


def _sigmoid(x):
    return 1.0 / (1.0 + jnp.exp(-x))


def _proj_kernel(x_ref, g_ref, w_ref, o_ref, h_ref):
    @pl.when(pl.program_id(1) == 0)
    def _():
        h_ref[...] = _rms(x_ref[...], g_ref[...]).astype(BF16)

    o_ref[...] = _dot(h_ref[...], w_ref[...]).astype(o_ref.dtype)


def _norm_proj(x, gain, w, *, tn, out_dtype):
    m, d = x.shape
    n = w.shape[1]
    tm = PROJ_TM
    return pl.pallas_call(
        _proj_kernel,
        grid=(m // tm, n // tn),
        in_specs=[pl.BlockSpec((tm, d), lambda i, j: (i, 0)),
                  pl.BlockSpec((1, d), lambda i, j: (0, 0)),
                  pl.BlockSpec((d, tn), lambda i, j: (0, j))],
        out_specs=pl.BlockSpec((tm, tn), lambda i, j: (i, j)),
        out_shape=jax.ShapeDtypeStruct((m, n), out_dtype),
        scratch_shapes=[pltpu.VMEM((tm, d), BF16)],
        compiler_params=pltpu.CompilerParams(
            dimension_semantics=("parallel", "arbitrary"), vmem_limit_bytes=VMEM_LIMIT),
        name="norm_proj",
    )(x, gain.reshape(1, d), w)


def _rope(y, c, s_lo, s_hi):
    return (y * c + pltpu.roll(y, LANES - ROPE_HALF, axis=1) * s_lo
            + pltpu.roll(y, ROPE_HALF, axis=1) * s_hi)


def _store_residue_major(y, n_rope, tabs, dilation, out_ref, scr_ref):
    tm, w = y.shape
    for b in range(w // LANES):
        sl = slice(b * LANES, (b + 1) * LANES)
        blk = _rope(y[:, sl], *tabs) if b < n_rope else y[:, sl]
        if dilation == 1:
            out_ref[0, 0, :, sl] = blk.astype(out_ref.dtype)
        else:
            scr_ref[b] = blk
    if dilation > 1:
        rows = tm // dilation
        for r in range(dilation):
            for b in range(w // LANES):
                out_ref[0, r, :, b * LANES:(b + 1) * LANES] = scr_ref[
                    b, pl.ds(r, rows, stride=dilation), :].astype(out_ref.dtype)


def _bproj_kernel(*refs, with_kv):
    if with_kv:
        (x_ref, gq_ref, wq_ref, gkv_ref, wkv_ref, c_ref, slo_ref, shi_ref,
         q0_ref, q1_ref, q2_ref, kv0_ref, kv1_ref, kv2_ref, scr_ref) = refs
    else:
        (x_ref, gq_ref, wq_ref, c_ref, slo_ref, shi_ref,
         q0_ref, q1_ref, q2_ref, scr_ref) = refs
    x = x_ref[...]
    xn = x * lax.rsqrt(jnp.mean(x * x, axis=-1, keepdims=True) + NORM_EPS)
    tabs = (c_ref[...], slo_ref[...], shi_ref[...])
    hq = (xn * gq_ref[...]).astype(BF16)
    for g, (q_ref, (_, dil)) in enumerate(zip((q0_ref, q1_ref, q2_ref), B_GROUPS)):
        y = _dot(hq, wq_ref[:, g * B_OUT:(g + 1) * B_OUT])
        _store_residue_major(y, B_HEADS, tabs, dil, q_ref, scr_ref)
    if with_kv:
        hkv = (xn * gkv_ref[...]).astype(BF16)
        for g, (kv_ref, (_, dil)) in enumerate(zip((kv0_ref, kv1_ref, kv2_ref), B_GROUPS)):
            y = _dot(hkv, wkv_ref[:, g * 2 * B_KW:(g + 1) * 2 * B_KW])
            _store_residue_major(y, B_KV_HEADS, tabs, dil, kv_ref, scr_ref)


def _b_projection(x, bsz, seq, rope, g_q, w_q, g_kv=None, w_kv=None):
    m, d = x.shape
    tm = PROJ_TM
    sblk = seq // tm
    with_kv = w_kv is not None

    def whole(a):
        return pl.BlockSpec(a.shape, lambda i: (0, 0))

    def out(width, dil):
        return pl.BlockSpec((1, dil, tm // dil, width), lambda i: (i // sblk, 0, i % sblk, 0))

    def out_shape(width, dil):
        return jax.ShapeDtypeStruct((bsz, dil, seq // dil, width), BF16)

    tab = pl.BlockSpec((tm, LANES), lambda i: (i % sblk, 0))
    args = [x, g_q.reshape(1, d), w_q]
    in_specs = [pl.BlockSpec((tm, d), lambda i: (i, 0)), whole(args[1]), whole(w_q)]
    if with_kv:
        args += [g_kv.reshape(1, d), w_kv]
        in_specs += [whole(args[3]), whole(w_kv)]
    args += list(rope)
    in_specs += [tab, tab, tab]
    widths = [B_OUT] * B_N_GROUPS + ([2 * B_KW] * B_N_GROUPS if with_kv else [])
    dils = [dil for _, dil in B_GROUPS] * (2 if with_kv else 1)
    return pl.pallas_call(
        functools.partial(_bproj_kernel, with_kv=with_kv),
        grid=(m // tm,),
        in_specs=in_specs,
        out_specs=[out(w, dl) for w, dl in zip(widths, dils)],
        out_shape=[out_shape(w, dl) for w, dl in zip(widths, dils)],
        scratch_shapes=[pltpu.VMEM((B_HEADS, tm, LANES), F32)],
        compiler_params=pltpu.CompilerParams(
            dimension_semantics=("parallel",), vmem_limit_bytes=VMEM_LIMIT),
        name="attn_projection",
    )(*args)


def _conv_silu(x, prev8, w):
    row8 = lax.broadcasted_iota(jnp.int32, (8, LANES), 0)
    acc = x * w[A_CONV - 1:A_CONV, :]
    for s in range(1, A_CONV):
        rolled = pltpu.roll(x, s, axis=0)
        head = jnp.where(row8 < s, pltpu.roll(prev8, s, axis=0), rolled[0:8])
        shifted = jnp.concatenate([head, rolled[8:]], axis=0)
        acc = acc + shifted * w[A_CONV - 1 - s:A_CONV - s, :]
    return acc * _sigmoid(acc)


def _l2(x):
    return x * lax.rsqrt(jnp.sum(x * x, axis=-1, keepdims=True) + NORM_EPS)


def _gdn_kernel(q_ref, k_ref, v_ref, z_ref, gate_ref, cw_ref, alog_ref, dt_ref, onorm_ref,
                o_ref, state_ref, carry_ref, gt_ref, gs_ref, lhs_ref, u_ref, attn_ref, kdt_ref):
    ts = q_ref.shape[1]
    c = A_CHUNK
    nc = ts // c
    nh = A_HEADS

    @pl.when(pl.program_id(1) == 0)
    def _():
        state_ref[...] = jnp.zeros_like(state_ref)
        carry_ref[...] = jnp.zeros_like(carry_ref)

    gates = gate_ref[0]
    a = gates + dt_ref[...]
    softplus = jnp.maximum(a, 0.0) + jnp.log(1.0 + jnp.exp(-jnp.abs(a)))
    g = -jnp.exp(alog_ref[...]) * softplus
    row_in_chunk = lax.broadcasted_iota(jnp.int32, (ts, LANES), 0) & (c - 1)
    s = 1
    while s < c:
        g = g + jnp.where(row_in_chunk >= s, pltpu.roll(g, s, axis=0), 0.0)
        s *= 2
    gt_ref[...] = g.T
    gs_ref[0] = g
    gs_ref[1] = _sigmoid(gates)

    ri = lax.broadcasted_iota(jnp.int32, (c, c), 0)
    ci = lax.broadcasted_iota(jnp.int32, (c, c), 1)
    incl = ri >= ci
    strict = ri > ci
    eye = jnp.where(ri == ci, 1.0, 0.0)

    for h0 in range(0, nh, GDN_GROUP):
        qkv = {}
        for h in range(h0, h0 + GDN_GROUP):
            hs = slice(h * A_DIM, (h + 1) * A_DIM)
            outs = []
            for t, ref in enumerate((q_ref, k_ref, v_ref)):
                raw = ref[0, :, hs]
                ws = slice(t * A_QK + h * A_DIM, t * A_QK + (h + 1) * A_DIM)
                outs.append(_conv_silu(raw, carry_ref[t, :, hs], cw_ref[:, ws]))
                carry_ref[t, :, hs] = raw[ts - 8:ts, :]
            qkv[h] = (_l2(outs[0]) * (A_DIM ** -0.5), _l2(outs[1]), outs[2])
        units = [(h, n) for h in range(h0, h0 + GDN_GROUP) for n in range(nc)]
        inv, power, rhs, qdec = {}, {}, {}, {}
        for (h, n) in units:
            sl = slice(n * c, (n + 1) * c)
            q, k, v = qkv[h]
            qc, kc, vc = q[sl], k[sl], v[sl]
            gc = gs_ref[0, sl, nh + h:nh + h + 1]
            bc = gs_ref[1, sl, h:h + 1]
            g_row = gt_ref[nh + h:nh + h + 1, sl]
            kb = kc * bc
            qk = _dot_nt(jnp.concatenate([kb, qc], axis=0).astype(BF16), kc.astype(BF16))
            decay = jnp.exp(jnp.where(incl, gc - g_row, -jnp.inf))
            lower = jnp.where(strict, qk[:c] * decay, 0.0)
            attn_ref[h, n] = jnp.where(incl, qk[c:] * decay, 0.0).astype(BF16)
            egc = jnp.exp(gc)
            rhs[h, n] = jnp.concatenate([vc * bc, kb * egc], axis=1).astype(BF16)
            qdec[h, n] = (qc * egc).astype(BF16)
            k_dec = kc * jnp.exp(gc[c - 1:c, :] - gc)
            kdt_ref[h, n] = k_dec.T.astype(BF16)
            inv[h, n] = eye - lower
            power[h, n] = lower.astype(BF16)
        p = 2
        while p < c:
            sq = {u: _dot(power[u], power[u]) for u in units}
            for u in units:
                power[u] = sq[u].astype(BF16)
                inv[u] = inv[u] + _dot(inv[u].astype(BF16), power[u])
            p *= 2
        for (h, n) in units:
            uw = _dot(inv[h, n].astype(BF16), rhs[h, n])
            u_ref[h, n] = uw[:, :A_DIM]
            lhs_ref[h, n] = jnp.concatenate([uw[:, A_DIM:].astype(BF16), qdec[h, n]], axis=0)

    onorm = onorm_ref[...]
    for n in range(nc):
        sl = slice(n * c, (n + 1) * c)
        ws = [_dot(lhs_ref[h, n], state_ref[h].astype(BF16)) for h in range(nh)]
        for h in range(nh):
            hs = slice(h * A_DIM, (h + 1) * A_DIM)
            v_new = (u_ref[h, n] - ws[h][:c]).astype(BF16)
            o = ws[h][c:] + _dot(attn_ref[h, n], v_new)
            g_last = gs_ref[0, (n + 1) * c - 1:(n + 1) * c, nh + h:nh + h + 1]
            state_ref[h] = state_ref[h] * jnp.exp(g_last) + _dot(kdt_ref[h, n], v_new)
            zc = z_ref[0, sl, hs]
            o = o * lax.rsqrt(jnp.mean(o * o, axis=-1, keepdims=True) + NORM_EPS)
            o_ref[0, sl, hs] = (o * onorm * (zc * _sigmoid(zc))).astype(o_ref.dtype)


def _gated_deltanet(proj, conv_w, a_log, dt_bias, out_norm):
    bsz, seq, _ = proj.shape
    ts = GDN_TS
    nh = A_HEADS
    nc = ts // A_CHUNK

    def cols(j):
        return pl.BlockSpec((1, ts, A_QK), lambda b, s: (b, s, j))

    row = pl.BlockSpec((1, LANES), lambda b, s: (0, 0))
    pad = jnp.zeros((LANES - 2 * nh,), F32)
    alog_row = jnp.concatenate([jnp.zeros((nh,), F32), a_log, pad]).reshape(1, LANES)
    dt_row = jnp.concatenate([jnp.zeros((nh,), F32), dt_bias, pad]).reshape(1, LANES)
    return pl.pallas_call(
        _gdn_kernel,
        grid=(bsz, seq // ts),
        in_specs=[cols(0), cols(1), cols(2), cols(3),
                  pl.BlockSpec((1, ts, LANES), lambda b, s: (b, s, 4 * nh)),
                  pl.BlockSpec(conv_w.shape, lambda b, s: (0, 0)), row, row, row],
        out_specs=pl.BlockSpec((1, ts, A_QK), lambda b, s: (b, s, 0)),
        out_shape=jax.ShapeDtypeStruct((bsz, seq, A_QK), BF16),
        scratch_shapes=[pltpu.VMEM((nh, A_DIM, A_DIM), F32),
                        pltpu.VMEM((3, 8, A_QK), F32),
                        pltpu.VMEM((LANES, ts), F32),
                        pltpu.VMEM((2, ts, LANES), F32),
                        pltpu.VMEM((nh, nc, 2 * A_CHUNK, A_DIM), BF16),
                        pltpu.VMEM((nh, nc, A_CHUNK, A_DIM), F32),
                        pltpu.VMEM((nh, nc, A_CHUNK, A_CHUNK), BF16),
                        pltpu.VMEM((nh, nc, A_DIM, A_CHUNK), BF16)],
        compiler_params=pltpu.CompilerParams(
            dimension_semantics=("parallel", "arbitrary"), vmem_limit_bytes=VMEM_LIMIT),
        name="gated_delta_rule",
    )(proj, proj, proj, proj, proj, conv_w, alog_row, dt_row, out_norm.reshape(1, LANES))


def _attn_kernel(q_ref, kvc_ref, kvp_ref, o_ref, lse_ref):
    nblk = pl.program_id(2)
    blk = B_BLK
    dh = B_HEAD_DIM
    rows = B_REP * blk
    kv_all = jnp.concatenate([kvp_ref[0, 0], kvc_ref[0, 0]], axis=0)
    qi = lax.broadcasted_iota(jnp.int32, (rows, 2 * blk), 0) & (blk - 1)
    kj = lax.broadcasted_iota(jnp.int32, (rows, 2 * blk), 1)
    dist = qi + blk - kj
    mask = (dist >= 0) & (dist <= blk) & ((kj >= blk) | (nblk > 0))
    lane = lax.broadcasted_iota(jnp.int32, (blk, LANES), 1)
    lse_tile = jnp.zeros((blk, LANES), F32)
    for g in range(B_KV_HEADS):
        qs = jnp.concatenate(
            [q_ref[0, 0, :, (g * B_REP + r) * dh:(g * B_REP + r + 1) * dh] for r in range(B_REP)],
            axis=0)
        s = _dot_nt(qs, kv_all[:, g * dh:(g + 1) * dh]) * (dh ** -0.5)
        s = jnp.where(mask, s, -jnp.inf)
        m = jnp.max(s, axis=-1, keepdims=True)
        e = jnp.exp(s - m)
        den = jnp.sum(e, axis=-1, keepdims=True)
        o = _dot(e.astype(BF16), kv_all[:, B_KW + g * dh:B_KW + (g + 1) * dh]) / den
        lse = m + jnp.log(den)
        for r in range(B_REP):
            hd = g * B_REP + r
            o_ref[0, 0, :, hd * dh:(hd + 1) * dh] = o[r * blk:(r + 1) * blk]
            lse_tile = jnp.where(lane == hd, lse[r * blk:(r + 1) * blk], lse_tile)
    lse_ref[0, 0] = lse_tile


def _dilated_attention(q, kv, gi):
    bsz, dil, length, _ = q.shape
    nb = length // B_BLK
    return pl.pallas_call(
        _attn_kernel,
        grid=(bsz, dil, nb),
        in_specs=[pl.BlockSpec((1, 1, B_BLK, B_OUT), lambda b, r, n: (b, r, n, 0)),
                  pl.BlockSpec((1, 1, B_BLK, 2 * B_KW), lambda b, r, n: (b, r, n, 0)),
                  pl.BlockSpec((1, 1, B_BLK, 2 * B_KW),
                               lambda b, r, n: (b, r, jnp.maximum(n - 1, 0), 0))],
        out_specs=[pl.BlockSpec((1, 1, B_BLK, B_OUT), lambda b, r, n: (b, r, n, 0)),
                   pl.BlockSpec((1, 1, B_BLK, LANES), lambda b, r, n: (b, r, n, 0))],
        out_shape=[jax.ShapeDtypeStruct((bsz, dil, length, B_OUT), F32),
                   jax.ShapeDtypeStruct((bsz, dil, length, LANES), F32)],
        compiler_params=pltpu.CompilerParams(
            dimension_semantics=("parallel", "parallel", "arbitrary"),
            vmem_limit_bytes=VMEM_LIMIT),
        name=f"dilated_attention_g{gi}",
    )(q, kv, kv)


def _merge_groups(o_refs, lse_refs, wts_ref, mixed_ref):
    tm = mixed_ref.shape[1]
    dils = [r.shape[1] for r in o_refs]
    for g, dil in enumerate(dils):
        rows = tm // dil
        for r in range(dil):
            wts_ref[g, pl.ds(r, rows, stride=dil), :] = lse_refs[g][0, r]
    lses = [wts_ref[g] for g in range(len(dils))]
    mx = functools.reduce(jnp.maximum, lses)
    wts = [jnp.exp(l - mx) for l in lses]
    inv = 1.0 / functools.reduce(jnp.add, wts)
    for g in range(len(dils)):
        wts_ref[g] = wts[g] * inv
    for g, dil in enumerate(dils):
        rows = tm // dil
        for r in range(dil):
            idx = pl.ds(r, rows, stride=dil)
            w = wts_ref[g, idx, :]
            for hd in range(B_HEADS):
                sl = slice(hd * B_HEAD_DIM, (hd + 1) * B_HEAD_DIM)
                term = w[:, hd:hd + 1] * o_refs[g][0, r, :, sl]
                if g == 0:
                    mixed_ref[hd, idx, :] = term
                else:
                    mixed_ref[hd, idx, :] += term


def _tail_kernel(*refs, n_mix, final):
    x_ref = refs[0]
    o_refs = refs[1:1 + n_mix]
    n_lse = n_mix if n_mix > 1 else 0
    lse_refs = refs[1 + n_mix:1 + n_mix + n_lse]
    rest = refs[1 + n_mix + n_lse:]
    (p_ref, wo_ref, gm_ref, wup_ref, wdn_ref, wg_ref, wp_ref, gf_ref, out_ref,
     x1_ref, h_ref, acc_ref) = rest[:12]
    f = pl.program_id(1)

    @pl.when(f == 0)
    def _():
        if n_mix > 1:
            wts_ref, mixed_ref = rest[12:]
            _merge_groups(o_refs, lse_refs, wts_ref, mixed_ref)
            mixed = jnp.concatenate(
                [mixed_ref[hd].astype(BF16) for hd in range(B_HEADS)], axis=1)
        else:
            mixed = o_refs[0][...]
        x1 = x_ref[...] + _dot(mixed, wo_ref[...])
        x1_ref[...] = x1
        h_ref[...] = _rms(x1, gm_ref[...]).astype(BF16)
        acc_ref[...] = jnp.zeros_like(acc_ref)

    u = jnp.maximum(_dot(h_ref[...], wup_ref[...]), 0.0)
    acc_ref[...] += _dot((u * u).astype(BF16), wdn_ref[...])

    @pl.when(f == pl.num_programs(1) - 1)
    def _():
        x2 = x1_ref[...] + acc_ref[...]
        gate = _sigmoid(_dot(x2.astype(BF16), wg_ref[...]))
        x3 = x2 + gate * _dot(p_ref[...].astype(BF16), wp_ref[...])
        if final:
            x3 = _rms(x3, gf_ref[...])
        out_ref[...] = x3


def _layer_tail(x, seq, mix, lses, p, w_o, g_mlp, w_up, w_dn, w_gate, w_ple, g_final, *, final):
    m, d = x.shape
    tm, tf = TAIL_TM, TAIL_TF
    sblk = seq // tm
    n_mix = len(mix)

    def rows(width):
        return pl.BlockSpec((tm, width), lambda i, f: (i, 0))

    def whole(a):
        return pl.BlockSpec(a.shape, lambda i, f: (0, 0))

    def residue_major(a):
        dil, width = a.shape[1], a.shape[3]
        return pl.BlockSpec((1, dil, tm // dil, width),
                            lambda i, f: (i // sblk, 0, i % sblk, 0))

    g_mlp = g_mlp.reshape(1, d)
    g_final = g_final.reshape(1, d)
    if n_mix > 1:
        mix_specs = [residue_major(a) for a in list(mix) + list(lses)]
        scratch = [pltpu.VMEM((n_mix, tm, LANES), F32),
                   pltpu.VMEM((B_HEADS, tm, B_HEAD_DIM), F32)]
    else:
        mix_specs = [rows(d)]
        scratch = []
    in_specs = ([rows(d)] + mix_specs
                + [rows(PLE_DIM), whole(w_o), whole(g_mlp),
                   pl.BlockSpec((d, tf), lambda i, f: (0, f)),
                   pl.BlockSpec((tf, d), lambda i, f: (f, 0)),
                   whole(w_gate), whole(w_ple), whole(g_final)])
    return pl.pallas_call(
        functools.partial(_tail_kernel, n_mix=n_mix, final=final),
        grid=(m // tm, D_FF // tf),
        in_specs=in_specs,
        out_specs=rows(d),
        out_shape=jax.ShapeDtypeStruct((m, d), F32),
        scratch_shapes=[pltpu.VMEM((tm, d), F32), pltpu.VMEM((tm, d), BF16),
                        pltpu.VMEM((tm, d), F32)] + scratch,
        compiler_params=pltpu.CompilerParams(
            dimension_semantics=("parallel", "arbitrary"), vmem_limit_bytes=VMEM_LIMIT),
        name="layer_tail",
    )(x, *mix, *lses, p, w_o, g_mlp, w_up, w_dn, w_gate, w_ple, g_final)


def _rope_tables(seq):
    inv = jnp.power(jnp.float32(ROPE_THETA), -jnp.arange(0, ROPE_DIM, 2, dtype=F32) / ROPE_DIM)
    ang = jnp.arange(seq, dtype=F32)[:, None] * inv[None, :]
    cos, sin = jnp.cos(ang), jnp.sin(ang)
    zeros = jnp.zeros((seq, LANES - ROPE_DIM), F32)
    zhalf = jnp.zeros((seq, ROPE_HALF), F32)
    c = jnp.concatenate([cos, cos, jnp.ones_like(zeros)], axis=1)
    s_lo = jnp.concatenate([-sin, zhalf, zeros], axis=1)
    s_hi = jnp.concatenate([zhalf, sin, zeros], axis=1)
    return c, s_lo, s_hi


def _group_kv_columns(w_kv):
    d = w_kv.shape[0]
    w = w_kv.reshape(d, 2, B_N_GROUPS, B_KW)
    return jnp.transpose(w, (0, 2, 1, 3)).reshape(d, 2 * B_N_GROUPS * B_KW)


def kernel(x, p, attn_norm, mlp_norm, a_w_in, a_conv_w, a_log, a_dt_bias, a_out_norm, a_w_out,
           kv_norm, b_w_kv, b_w_q, b_w_o, mlp_w_up, mlp_w_down, ple_w_proj, ple_w_gate, final_norm):
    bsz, seq, d = x.shape
    m = bsz * seq
    xf = x.reshape(m, d)
    rope = _rope_tables(seq)
    kvs = None
    for i in range(DEPTH):
        if i < N_A_LAYERS:
            w_in = jnp.pad(a_w_in[i], ((0, 0), (0, A_IN_PAD - A_IN_COLS))).astype(BF16)
            proj = _norm_proj(xf, attn_norm[i], w_in, tn=A_IN_PAD // 3, out_dtype=F32)
            o = _gated_deltanet(proj.reshape(bsz, seq, A_IN_PAD), a_conv_w[i], a_log[i],
                                a_dt_bias[i], a_out_norm[i])
            mix, lses, w_o = [o.reshape(m, A_QK)], [], a_w_out[i]
        else:
            j = i - N_A_LAYERS
            w_q = b_w_q[j].astype(BF16)
            if kvs is None:
                outs = _b_projection(xf, bsz, seq, rope, attn_norm[i], w_q, kv_norm,
                                     _group_kv_columns(b_w_kv).astype(BF16))
                qs, kvs = outs[:B_N_GROUPS], outs[B_N_GROUPS:]
            else:
                qs = _b_projection(xf, bsz, seq, rope, attn_norm[i], w_q)
            mix, lses = [], []
            for gi in range(B_N_GROUPS):
                o_g, lse_g = _dilated_attention(qs[gi], kvs[gi], gi)
                mix.append(o_g)
                lses.append(lse_g)
            w_o = b_w_o[j]
        xf = _layer_tail(xf, seq, mix, lses, p[i].reshape(m, PLE_DIM), w_o.astype(BF16),
                         mlp_norm[i], mlp_w_up[i].astype(BF16), mlp_w_down[i].astype(BF16),
                         ple_w_gate[i].astype(BF16), ple_w_proj[i].astype(BF16), final_norm,
                         final=(i == DEPTH - 1))
    return xf.reshape(bsz, seq, d)
```

```python
import functools

import jax
import jax.numpy as jnp
from jax import lax
from jax.experimental import pallas as pl
from jax.experimental.pallas import tpu as pltpu

F32 = jnp.float32
BF16 = jnp.bfloat16

D_MODEL = 1024
DEPTH = 4
N_A_LAYERS = DEPTH // 2
PLE_DIM = 256
D_FF = 4 * D_MODEL
NORM_EPS = 1e-6

A_HEADS = 8
A_DIM = 128
A_CONV = 4
A_CHUNK = 64
A_QK = A_HEADS * A_DIM
A_IN_COLS = 4 * A_QK + 2 * A_HEADS
LANES = 128
A_IN_PAD = 4 * A_QK + LANES

B_HEAD_DIM = 128
B_GROUPS = ((128, 1), (512, 4), (2048, 16))
B_N_GROUPS = len(B_GROUPS)
B_HEADS = 8
B_KV_HEADS = 2
B_REP = B_HEADS // B_KV_HEADS
B_BLK = 128
B_OUT = B_HEADS * B_HEAD_DIM
B_KW = B_KV_HEADS * B_HEAD_DIM
ROPE_THETA = 500000.0
ROPE_DIM = B_HEAD_DIM // 4
ROPE_HALF = ROPE_DIM // 2

VMEM_LIMIT = 56 * 1024 * 1024

PROJ_TM = 512
A_PROJ_TN = 256
GDN_TS = 256
GDN_GROUP = 4
ATTN_QBLOCKS = 4
LOG2E = 1.4426950408889634
TAIL_TM = 512
TAIL_TF = 1024


def _dot(a, b):
    return jnp.dot(a, b, preferred_element_type=F32)


def _dot_nt(a, b):
    return lax.dot_general(a, b, (((1,), (1,)), ((), ())), preferred_element_type=F32)


def _rms(x, gain):
    return x * lax.rsqrt(jnp.mean(x * x, axis=-1, keepdims=True) + NORM_EPS) * gain


def _sigmoid(x):
    return 1.0 / (1.0 + jnp.exp(-x))


def _conv_silu(y, hist_ref, prev8, w):
    tm = y.shape[0]
    hist_ref[0:8, :] = prev8
    hist_ref[8:8 + tm, :] = y
    acc = y * w[A_CONV - 1:A_CONV, :]
    for s in range(1, A_CONV):
        acc = acc + hist_ref[8 - s:8 - s + tm, :] * w[A_CONV - 1 - s:A_CONV - s, :]
    return acc * _sigmoid(acc)


def _l2(x):
    return x * lax.rsqrt(jnp.sum(x * x, axis=-1, keepdims=True) + NORM_EPS)


def _aproj_kernel(x_ref, g_ref, w_ref, cw_ref, q_ref, k_ref, v_ref, z_ref, gate_ref,
                  carry_ref, hist_ref, *, sblk):
    tm = x_ref.shape[0]
    wd = A_PROJ_TN

    @pl.when(pl.program_id(0) % sblk == 0)
    def _():
        carry_ref[...] = jnp.zeros_like(carry_ref)

    h = _rms(x_ref[...], g_ref[...]).astype(BF16)
    for t, out_ref in enumerate((q_ref, k_ref, v_ref)):
        for j in range(A_QK // wd):
            cs = slice(j * wd, (j + 1) * wd)
            ws = slice(t * A_QK + j * wd, t * A_QK + (j + 1) * wd)
            y = _dot(h, w_ref[:, ws])
            act = _conv_silu(y, hist_ref, carry_ref[t, :, cs], cw_ref[:, ws])
            carry_ref[t, :, cs] = y[tm - 8:tm, :]
            for hd in range(wd // A_DIM):
                hs = slice(hd * A_DIM, (hd + 1) * A_DIM)
                piece = act[:, hs]
                if t == 0:
                    piece = _l2(piece) * (A_DIM ** -0.5)
                elif t == 1:
                    piece = _l2(piece)
                out_ref[:, j * wd + hd * A_DIM:j * wd + (hd + 1) * A_DIM] = piece
    for j in range(A_QK // wd):
        cs = slice(j * wd, (j + 1) * wd)
        z = _dot(h, w_ref[:, 3 * A_QK + j * wd:3 * A_QK + (j + 1) * wd])
        z_ref[:, cs] = z * _sigmoid(z)
    gate_ref[...] = _dot(h, w_ref[:, 4 * A_QK:])


def _a_projection(x, seq, gain, w_in, conv_w):
    m, d = x.shape
    tm = PROJ_TM
    rows = pl.BlockSpec((tm, A_QK), lambda i: (i, 0))
    act = jax.ShapeDtypeStruct((m, A_QK), F32)
    return pl.pallas_call(
        functools.partial(_aproj_kernel, sblk=seq // tm),
        grid=(m // tm,),
        in_specs=[pl.BlockSpec((tm, d), lambda i: (i, 0)),
                  pl.BlockSpec((1, d), lambda i: (0, 0)),
                  pl.BlockSpec(w_in.shape, lambda i: (0, 0)),
                  pl.BlockSpec(conv_w.shape, lambda i: (0, 0))],
        out_specs=[rows, rows, rows, rows, pl.BlockSpec((tm, LANES), lambda i: (i, 0))],
        out_shape=[act, act, act, act, jax.ShapeDtypeStruct((m, LANES), F32)],
        scratch_shapes=[pltpu.VMEM((3, 8, A_QK), F32),
                        pltpu.VMEM((tm + 8, A_PROJ_TN), F32)],
        compiler_params=pltpu.CompilerParams(
            dimension_semantics=("arbitrary",), vmem_limit_bytes=VMEM_LIMIT),
        name="gdn_projection",
    )(x, gain.reshape(1, d), w_in, conv_w)


def _rope(y, c, s_lo, s_hi):
    return (y * c + pltpu.roll(y, LANES - ROPE_HALF, axis=1) * s_lo
            + pltpu.roll(y, ROPE_HALF, axis=1) * s_hi)


def _store_residue_major(y, n_rope, tabs, dilation, out_ref, scr_ref):
    tm, w = y.shape
    for b in range(w // LANES):
        sl = slice(b * LANES, (b + 1) * LANES)
        blk = _rope(y[:, sl], *tabs) if b < n_rope else y[:, sl]
        if dilation == 1:
            out_ref[0, 0, :, sl] = blk.astype(out_ref.dtype)
        else:
            scr_ref[b] = blk
    if dilation > 1:
        rows = tm // dilation
        for r in range(dilation):
            for b in range(w // LANES):
                out_ref[0, r, :, b * LANES:(b + 1) * LANES] = scr_ref[
                    b, pl.ds(r, rows, stride=dilation), :].astype(out_ref.dtype)


def _bproj_kernel(*refs, with_kv):
    if with_kv:
        (x_ref, gq_ref, wq_ref, gkv_ref, wkv_ref, c_ref, slo_ref, shi_ref,
         q0_ref, q1_ref, q2_ref, kv0_ref, kv1_ref, kv2_ref, scr_ref) = refs
    else:
        (x_ref, gq_ref, wq_ref, c_ref, slo_ref, shi_ref,
         q0_ref, q1_ref, q2_ref, scr_ref) = refs
    x = x_ref[...]
    xn = x * lax.rsqrt(jnp.mean(x * x, axis=-1, keepdims=True) + NORM_EPS)
    tabs = (c_ref[...], slo_ref[...], shi_ref[...])
    hq = (xn * gq_ref[...]).astype(BF16)
    for g, (q_ref, (_, dil)) in enumerate(zip((q0_ref, q1_ref, q2_ref), B_GROUPS)):
        y = _dot(hq, wq_ref[:, g * B_OUT:(g + 1) * B_OUT])
        _store_residue_major(y, B_HEADS, tabs, dil, q_ref, scr_ref)
    if with_kv:
        hkv = (xn * gkv_ref[...]).astype(BF16)
        for g, (kv_ref, (_, dil)) in enumerate(zip((kv0_ref, kv1_ref, kv2_ref), B_GROUPS)):
            y = _dot(hkv, wkv_ref[:, g * 2 * B_KW:(g + 1) * 2 * B_KW])
            _store_residue_major(y, B_KV_HEADS, tabs, dil, kv_ref, scr_ref)


def _b_projection(x, bsz, seq, rope, g_q, w_q, g_kv=None, w_kv=None):
    m, d = x.shape
    tm = PROJ_TM
    sblk = seq // tm
    with_kv = w_kv is not None

    def whole(a):
        return pl.BlockSpec(a.shape, lambda i: (0, 0))

    def out(width, dil):
        return pl.BlockSpec((1, dil, tm // dil, width), lambda i: (i // sblk, 0, i % sblk, 0))

    def out_shape(width, dil):
        return jax.ShapeDtypeStruct((bsz, dil, seq // dil, width), BF16)

    tab = pl.BlockSpec((tm, LANES), lambda i: (i % sblk, 0))
    args = [x, g_q.reshape(1, d), w_q]
    in_specs = [pl.BlockSpec((tm, d), lambda i: (i, 0)), whole(args[1]), whole(w_q)]
    if with_kv:
        args += [g_kv.reshape(1, d), w_kv]
        in_specs += [whole(args[3]), whole(w_kv)]
    args += list(rope)
    in_specs += [tab, tab, tab]
    widths = [B_OUT] * B_N_GROUPS + ([2 * B_KW] * B_N_GROUPS if with_kv else [])
    dils = [dil for _, dil in B_GROUPS] * (2 if with_kv else 1)
    return pl.pallas_call(
        functools.partial(_bproj_kernel, with_kv=with_kv),
        grid=(m // tm,),
        in_specs=in_specs,
        out_specs=[out(w, dl) for w, dl in zip(widths, dils)],
        out_shape=[out_shape(w, dl) for w, dl in zip(widths, dils)],
        scratch_shapes=[pltpu.VMEM((B_HEADS, tm, LANES), F32)],
        compiler_params=pltpu.CompilerParams(
            dimension_semantics=("parallel",), vmem_limit_bytes=VMEM_LIMIT),
        name="attn_projection",
    )(*args)


_G_BETA, _G_CUM, _G_EXP, _G_KDEC, _G_LAST = (i * A_HEADS for i in range(5))


def _gdn_kernel(q_ref, k_ref, v_ref, z_ref, gate_ref, alog_ref, dt_ref, onorm_ref,
                o_ref, state_ref, gt_ref, gs_ref, lhs_ref, u_ref, attn_ref, kdt_ref):
    ts = q_ref.shape[1]
    c = A_CHUNK
    nc = ts // c
    nh = A_HEADS

    @pl.when(pl.program_id(1) == 0)
    def _():
        state_ref[...] = jnp.zeros_like(state_ref)

    gates_t = gate_ref[0].T
    a = gates_t[nh:2 * nh] + dt_ref[...]
    softplus = jnp.maximum(a, 0.0) + jnp.log(1.0 + jnp.exp(-jnp.abs(a)))
    g = -jnp.exp(alog_ref[...]) * softplus
    lane_in_chunk = lax.broadcasted_iota(jnp.int32, (nh, ts), 1) & (c - 1)
    s = 1
    while s < c:
        g = g + jnp.where(lane_in_chunk >= s, pltpu.roll(g, s, axis=1), 0.0)
        s *= 2
    g_last = jnp.concatenate(
        [jnp.broadcast_to(g[:, (n + 1) * c - 1:(n + 1) * c], (nh, c)) for n in range(nc)], axis=1)
    rows = jnp.concatenate(
        [_sigmoid(gates_t[0:nh]), g, jnp.exp(g), jnp.exp(g_last - g), jnp.exp(g_last),
         jnp.zeros((LANES - 5 * nh, ts), F32)], axis=0)
    gt_ref[...] = g
    gs_ref[...] = rows.T

    ri = lax.broadcasted_iota(jnp.int32, (c, c), 0)
    ci = lax.broadcasted_iota(jnp.int32, (c, c), 1)
    incl = ri >= ci
    strict = ri > ci
    eye = jnp.where(ri == ci, 1.0, 0.0)

    for h0 in range(0, nh, GDN_GROUP):
        units = [(h, n) for h in range(h0, h0 + GDN_GROUP) for n in range(nc)]
        inv, power, rhs, qdec = {}, {}, {}, {}
        for (h, n) in units:
            sl = slice(n * c, (n + 1) * c)
            hs = slice(h * A_DIM, (h + 1) * A_DIM)
            qc, kc, vc = q_ref[0, sl, hs], k_ref[0, sl, hs], v_ref[0, sl, hs]
            bc = gs_ref[sl, _G_BETA + h:_G_BETA + h + 1]
            gc = gs_ref[sl, _G_CUM + h:_G_CUM + h + 1]
            egc = gs_ref[sl, _G_EXP + h:_G_EXP + h + 1]
            kdf = gs_ref[sl, _G_KDEC + h:_G_KDEC + h + 1]
            g_row = gt_ref[h:h + 1, sl]
            kb = kc * bc
            qk = _dot_nt(jnp.concatenate([kb, qc], axis=0).astype(BF16), kc.astype(BF16))
            decay = jnp.exp(jnp.where(incl, gc - g_row, -jnp.inf))
            lower = jnp.where(strict, qk[:c] * decay, 0.0)
            attn_ref[h, n] = jnp.where(incl, qk[c:] * decay, 0.0).astype(BF16)
            rhs[h, n] = jnp.concatenate([vc * bc, kb * egc], axis=1).astype(BF16)
            qdec[h, n] = (qc * egc).astype(BF16)
            kdt_ref[h, n] = (kc * kdf).T.astype(BF16)
            inv[h, n] = eye - lower
            power[h, n] = lower.astype(BF16)
        p = 2
        while p < c:
            sq = {u: _dot(power[u], power[u]) for u in units}
            for u in units:
                power[u] = sq[u].astype(BF16)
                inv[u] = inv[u] + _dot(inv[u].astype(BF16), power[u])
            p *= 2
        for (h, n) in units:
            uw = _dot(inv[h, n].astype(BF16), rhs[h, n])
            u_ref[h, n] = uw[:, :A_DIM]
            lhs_ref[h, n] = jnp.concatenate([uw[:, A_DIM:].astype(BF16), qdec[h, n]], axis=0)

    onorm = onorm_ref[...]
    for n in range(nc):
        sl = slice(n * c, (n + 1) * c)
        ws = [_dot(lhs_ref[h, n], state_ref[h].astype(BF16)) for h in range(nh)]
        for h in range(nh):
            hs = slice(h * A_DIM, (h + 1) * A_DIM)
            v_new = (u_ref[h, n] - ws[h][:c]).astype(BF16)
            o = ws[h][c:] + _dot(attn_ref[h, n], v_new)
            chunk_decay = gs_ref[n * c:n * c + 1, _G_LAST + h:_G_LAST + h + 1]
            state_ref[h] = state_ref[h] * chunk_decay + _dot(kdt_ref[h, n], v_new)
            o = o * lax.rsqrt(jnp.mean(o * o, axis=-1, keepdims=True) + NORM_EPS)
            o_ref[0, sl, hs] = (o * onorm * z_ref[0, sl, hs]).astype(o_ref.dtype)


def _gated_deltanet(q, k, v, z, gates, a_log, dt_bias, out_norm):
    bsz, seq, _ = q.shape
    ts = GDN_TS
    nh = A_HEADS
    nc = ts // A_CHUNK
    act = pl.BlockSpec((1, ts, A_QK), lambda b, s: (b, s, 0))
    col = pl.BlockSpec((nh, 1), lambda b, s: (0, 0))
    return pl.pallas_call(
        _gdn_kernel,
        grid=(bsz, seq // ts),
        in_specs=[act, act, act, act, pl.BlockSpec((1, ts, LANES), lambda b, s: (b, s, 0)),
                  col, col, pl.BlockSpec((1, LANES), lambda b, s: (0, 0))],
        out_specs=act,
        out_shape=jax.ShapeDtypeStruct((bsz, seq, A_QK), BF16),
        scratch_shapes=[pltpu.VMEM((nh, A_DIM, A_DIM), F32),
                        pltpu.VMEM((nh, ts), F32),
                        pltpu.VMEM((ts, LANES), F32),
                        pltpu.VMEM((nh, nc, 2 * A_CHUNK, A_DIM), BF16),
                        pltpu.VMEM((nh, nc, A_CHUNK, A_DIM), F32),
                        pltpu.VMEM((nh, nc, A_CHUNK, A_CHUNK), BF16),
                        pltpu.VMEM((nh, nc, A_DIM, A_CHUNK), BF16)],
        compiler_params=pltpu.CompilerParams(
            dimension_semantics=("parallel", "arbitrary"), vmem_limit_bytes=VMEM_LIMIT),
        name="gated_delta_rule",
    )(q, k, v, z, gates, a_log.reshape(nh, 1), dt_bias.reshape(nh, 1),
      out_norm.reshape(1, LANES))


def _attn_kernel(q_ref, kvc_ref, kvp_ref, bias_ref, o_ref, lse_ref, *, nq):
    step = pl.program_id(2)
    blk = B_BLK
    dh = B_HEAD_DIM
    scale = dh ** -0.5
    kv_all = jnp.concatenate([kvp_ref[0, 0], kvc_ref[0, 0]], axis=0)
    lane = lax.broadcasted_iota(jnp.int32, (blk, LANES), 1)
    for i in range(nq):
        rows = slice(i * blk, (i + 1) * blk)
        keys = kv_all[i * blk:(i + 2) * blk]
        bias = bias_ref[jnp.minimum(step, 1)] if i == 0 else bias_ref[1]
        lse_tile = jnp.zeros((blk, LANES), F32)
        for g in range(B_KV_HEADS):
            qs = jnp.concatenate(
                [q_ref[0, 0, rows, (g * B_REP + r) * dh:(g * B_REP + r + 1) * dh]
                 for r in range(B_REP)], axis=0)
            s = _dot_nt(qs, keys[:, g * dh:(g + 1) * dh]) + bias
            m = jnp.max(s, axis=-1, keepdims=True)
            e = jnp.exp2((s - m) * (scale * LOG2E))
            den = jnp.sum(e, axis=-1, keepdims=True)
            o = _dot(e.astype(BF16), keys[:, B_KW + g * dh:B_KW + (g + 1) * dh]) / den
            lse = m * scale + jnp.log(den)
            for r in range(B_REP):
                hd = g * B_REP + r
                o_ref[0, 0, rows, hd * dh:(hd + 1) * dh] = o[r * blk:(r + 1) * blk]
                lse_tile = jnp.where(lane == hd, lse[r * blk:(r + 1) * blk], lse_tile)
        lse_ref[0, 0, rows, :] = lse_tile


def _band_bias():
    qi = jnp.arange(B_REP * B_BLK)[:, None] & (B_BLK - 1)
    kj = jnp.arange(2 * B_BLK)[None, :]
    dist = qi + B_BLK - kj
    band = (dist >= 0) & (dist <= B_BLK)
    first = band & (kj >= B_BLK)
    return jnp.where(jnp.stack([first, band]), 0.0, -jnp.inf).astype(F32)


def _dilated_attention(q, kv, bias, gi):
    bsz, dil, length, _ = q.shape
    nq = min(ATTN_QBLOCKS, length // B_BLK)
    tq = nq * B_BLK
    return pl.pallas_call(
        functools.partial(_attn_kernel, nq=nq),
        grid=(bsz, dil, length // tq),
        in_specs=[pl.BlockSpec((1, 1, tq, B_OUT), lambda b, r, n: (b, r, n, 0)),
                  pl.BlockSpec((1, 1, tq, 2 * B_KW), lambda b, r, n: (b, r, n, 0)),
                  pl.BlockSpec((1, 1, B_BLK, 2 * B_KW),
                               lambda b, r, n: (b, r, jnp.maximum(nq * n - 1, 0), 0)),
                  pl.BlockSpec(bias.shape, lambda b, r, n: (0, 0, 0))],
        out_specs=[pl.BlockSpec((1, 1, tq, B_OUT), lambda b, r, n: (b, r, n, 0)),
                   pl.BlockSpec((1, 1, tq, LANES), lambda b, r, n: (b, r, n, 0))],
        out_shape=[jax.ShapeDtypeStruct((bsz, dil, length, B_OUT), F32),
                   jax.ShapeDtypeStruct((bsz, dil, length, LANES), F32)],
        compiler_params=pltpu.CompilerParams(
            dimension_semantics=("parallel", "parallel", "arbitrary"),
            vmem_limit_bytes=VMEM_LIMIT),
        name=f"dilated_attention_g{gi}",
    )(q, kv, kv, bias)


def _merge_groups(o_refs, lse_refs, wts_ref, mixed_ref):
    tm = mixed_ref.shape[1]
    dils = [r.shape[1] for r in o_refs]
    for g, dil in enumerate(dils):
        rows = tm // dil
        for r in range(dil):
            wts_ref[g, pl.ds(r, rows, stride=dil), :] = lse_refs[g][0, r]
    lses = [wts_ref[g] for g in range(len(dils))]
    mx = functools.reduce(jnp.maximum, lses)
    wts = [jnp.exp(l - mx) for l in lses]
    inv = 1.0 / functools.reduce(jnp.add, wts)
    for g in range(len(dils)):
        wts_ref[g] = wts[g] * inv
    for g, dil in enumerate(dils):
        rows = tm // dil
        for r in range(dil):
            idx = pl.ds(r, rows, stride=dil)
            w = wts_ref[g, idx, :]
            for hd in range(B_HEADS):
                sl = slice(hd * B_HEAD_DIM, (hd + 1) * B_HEAD_DIM)
                term = w[:, hd:hd + 1] * o_refs[g][0, r, :, sl]
                if g == 0:
                    mixed_ref[hd, idx, :] = term
                else:
                    mixed_ref[hd, idx, :] += term


def _tail_kernel(*refs, n_mix, final):
    x_ref = refs[0]
    o_refs = refs[1:1 + n_mix]
    n_lse = n_mix if n_mix > 1 else 0
    lse_refs = refs[1 + n_mix:1 + n_mix + n_lse]
    rest = refs[1 + n_mix + n_lse:]
    (p_ref, wo_ref, gm_ref, wup_ref, wdn_ref, wg_ref, wp_ref, gf_ref, out_ref,
     x1_ref, h_ref, acc_ref) = rest[:12]
    f = pl.program_id(1)

    @pl.when(f == 0)
    def _():
        if n_mix > 1:
            wts_ref, mixed_ref = rest[12:]
            _merge_groups(o_refs, lse_refs, wts_ref, mixed_ref)
            mixed = jnp.concatenate(
                [mixed_ref[hd].astype(BF16) for hd in range(B_HEADS)], axis=1)
        else:
            mixed = o_refs[0][...]
        x1 = x_ref[...] + _dot(mixed, wo_ref[...])
        x1_ref[...] = x1
        h_ref[...] = _rms(x1, gm_ref[...]).astype(BF16)
        acc_ref[...] = jnp.zeros_like(acc_ref)

    u = jnp.maximum(_dot(h_ref[...], wup_ref[...]), 0.0)
    acc_ref[...] += _dot((u * u).astype(BF16), wdn_ref[...])

    @pl.when(f == pl.num_programs(1) - 1)
    def _():
        x2 = x1_ref[...] + acc_ref[...]
        gate = _sigmoid(_dot(x2.astype(BF16), wg_ref[...]))
        x3 = x2 + gate * _dot(p_ref[...].astype(BF16), wp_ref[...])
        if final:
            x3 = _rms(x3, gf_ref[...])
        out_ref[...] = x3


def _layer_tail(x, seq, mix, lses, p, w_o, g_mlp, w_up, w_dn, w_gate, w_ple, g_final, *, final):
    m, d = x.shape
    tm, tf = TAIL_TM, TAIL_TF
    sblk = seq // tm
    n_mix = len(mix)

    def rows(width):
        return pl.BlockSpec((tm, width), lambda i, f: (i, 0))

    def whole(a):
        return pl.BlockSpec(a.shape, lambda i, f: (0, 0))

    def residue_major(a):
        dil, width = a.shape[1], a.shape[3]
        return pl.BlockSpec((1, dil, tm // dil, width),
                            lambda i, f: (i // sblk, 0, i % sblk, 0))

    g_mlp = g_mlp.reshape(1, d)
    g_final = g_final.reshape(1, d)
    if n_mix > 1:
        mix_specs = [residue_major(a) for a in list(mix) + list(lses)]
        scratch = [pltpu.VMEM((n_mix, tm, LANES), F32),
                   pltpu.VMEM((B_HEADS, tm, B_HEAD_DIM), F32)]
    else:
        mix_specs = [rows(d)]
        scratch = []
    in_specs = ([rows(d)] + mix_specs
                + [rows(PLE_DIM), whole(w_o), whole(g_mlp),
                   pl.BlockSpec((d, tf), lambda i, f: (0, f)),
                   pl.BlockSpec((tf, d), lambda i, f: (f, 0)),
                   whole(w_gate), whole(w_ple), whole(g_final)])
    return pl.pallas_call(
        functools.partial(_tail_kernel, n_mix=n_mix, final=final),
        grid=(m // tm, D_FF // tf),
        in_specs=in_specs,
        out_specs=rows(d),
        out_shape=jax.ShapeDtypeStruct((m, d), F32),
        scratch_shapes=[pltpu.VMEM((tm, d), F32), pltpu.VMEM((tm, d), BF16),
                        pltpu.VMEM((tm, d), F32)] + scratch,
        compiler_params=pltpu.CompilerParams(
            dimension_semantics=("parallel", "arbitrary"), vmem_limit_bytes=VMEM_LIMIT),
        name="layer_tail",
    )(x, *mix, *lses, p, w_o, g_mlp, w_up, w_dn, w_gate, w_ple, g_final)


def _rope_tables(seq):
    inv = jnp.power(jnp.float32(ROPE_THETA), -jnp.arange(0, ROPE_DIM, 2, dtype=F32) / ROPE_DIM)
    ang = jnp.arange(seq, dtype=F32)[:, None] * inv[None, :]
    cos, sin = jnp.cos(ang), jnp.sin(ang)
    zeros = jnp.zeros((seq, LANES - ROPE_DIM), F32)
    zhalf = jnp.zeros((seq, ROPE_HALF), F32)
    c = jnp.concatenate([cos, cos, jnp.ones_like(zeros)], axis=1)
    s_lo = jnp.concatenate([-sin, zhalf, zeros], axis=1)
    s_hi = jnp.concatenate([zhalf, sin, zeros], axis=1)
    return c, s_lo, s_hi


def _group_kv_columns(w_kv):
    d = w_kv.shape[0]
    w = w_kv.reshape(d, 2, B_N_GROUPS, B_KW)
    return jnp.transpose(w, (0, 2, 1, 3)).reshape(d, 2 * B_N_GROUPS * B_KW)


def kernel(x, p, attn_norm, mlp_norm, a_w_in, a_conv_w, a_log, a_dt_bias, a_out_norm, a_w_out,
           kv_norm, b_w_kv, b_w_q, b_w_o, mlp_w_up, mlp_w_down, ple_w_proj, ple_w_gate, final_norm):
    bsz, seq, d = x.shape
    m = bsz * seq
    xf = x.reshape(m, d)
    rope = _rope_tables(seq)
    bias = _band_bias()
    kvs = None
    for i in range(DEPTH):
        if i < N_A_LAYERS:
            w_in = jnp.pad(a_w_in[i], ((0, 0), (0, A_IN_PAD - A_IN_COLS))).astype(BF16)
            q, k, v, z, gates = _a_projection(xf, seq, attn_norm[i], w_in, a_conv_w[i])
            o = _gated_deltanet(*(t.reshape(bsz, seq, -1) for t in (q, k, v, z, gates)),
                                a_log[i], a_dt_bias[i], a_out_norm[i])
            mix, lses, w_o = [o.reshape(m, A_QK)], [], a_w_out[i]
        else:
            j = i - N_A_LAYERS
            w_q = b_w_q[j].astype(BF16)
            if kvs is None:
                outs = _b_projection(xf, bsz, seq, rope, attn_norm[i], w_q, kv_norm,
                                     _group_kv_columns(b_w_kv).astype(BF16))
                qs, kvs = outs[:B_N_GROUPS], outs[B_N_GROUPS:]
            else:
                qs = _b_projection(xf, bsz, seq, rope, attn_norm[i], w_q)
            mix, lses = [], []
            for gi in range(B_N_GROUPS):
                o_g, lse_g = _dilated_attention(qs[gi], kvs[gi], bias, gi)
                mix.append(o_g)
                lses.append(lse_g)
            w_o = b_w_o[j]
        xf = _layer_tail(xf, seq, mix, lses, p[i].reshape(m, PLE_DIM), w_o.astype(BF16),
                         mlp_norm[i], mlp_w_up[i].astype(BF16), mlp_w_down[i].astype(BF16),
                         ple_w_gate[i].astype(BF16), ple_w_proj[i].astype(BF16), final_norm,
                         final=(i == DEPTH - 1))
    return xf.reshape(bsz, seq, d)
```

```python
import functools

import jax
import jax.numpy as jnp
from jax import lax
from jax.experimental import pallas as pl
from jax.experimental.pallas import tpu as pltpu

F32 = jnp.float32
BF16 = jnp.bfloat16

D_MODEL = 1024
DEPTH = 4
N_A_LAYERS = DEPTH // 2
PLE_DIM = 256
D_FF = 4 * D_MODEL
NORM_EPS = 1e-6

A_HEADS = 8
A_DIM = 128
A_CONV = 4
A_CHUNK = 64
A_QK = A_HEADS * A_DIM
A_IN_COLS = 4 * A_QK + 2 * A_HEADS
LANES = 128
A_IN_PAD = 4 * A_QK + LANES

B_HEAD_DIM = 128
B_GROUPS = ((128, 1), (512, 4), (2048, 16))
B_N_GROUPS = len(B_GROUPS)
B_HEADS = 8
B_KV_HEADS = 2
B_REP = B_HEADS // B_KV_HEADS
B_BLK = 128
B_OUT = B_HEADS * B_HEAD_DIM
B_KW = B_KV_HEADS * B_HEAD_DIM
ROPE_THETA = 500000.0
ROPE_DIM = B_HEAD_DIM // 4
ROPE_HALF = ROPE_DIM // 2

VMEM_LIMIT = 56 * 1024 * 1024

PROJ_TM = 512
A_PROJ_TN = 256
GDN_TS = 256
GDN_GROUP = 4
ATTN_QBLOCKS = 4
LOG2E = 1.4426950408889634
TAIL_TM = 512
TAIL_TM_MERGE = 256
TAIL_TF = 512


def _dot(a, b):
    return jnp.dot(a, b, preferred_element_type=F32)


def _dot_nt(a, b):
    return lax.dot_general(a, b, (((1,), (1,)), ((), ())), preferred_element_type=F32)


def _rms(x, gain):
    return x * lax.rsqrt(jnp.mean(x * x, axis=-1, keepdims=True) + NORM_EPS) * gain


def _sigmoid(x):
    return 1.0 / (1.0 + jnp.exp(-x))


def _conv_silu(y, hist_ref, prev8, w):
    tm = y.shape[0]
    hist_ref[0:8, :] = prev8
    hist_ref[8:8 + tm, :] = y
    acc = y * w[A_CONV - 1:A_CONV, :]
    for s in range(1, A_CONV):
        acc = acc + hist_ref[8 - s:8 - s + tm, :] * w[A_CONV - 1 - s:A_CONV - s, :]
    return acc * _sigmoid(acc)


def _l2(x):
    return x * lax.rsqrt(jnp.sum(x * x, axis=-1, keepdims=True) + NORM_EPS)


def _aproj_kernel(x_ref, g_ref, w_ref, cw_ref, q_ref, k_ref, v_ref, z_ref, gate_ref,
                  carry_ref, hist_ref, *, sblk):
    tm = x_ref.shape[0]
    wd = A_PROJ_TN

    @pl.when(pl.program_id(0) % sblk == 0)
    def _():
        carry_ref[...] = jnp.zeros_like(carry_ref)

    h = _rms(x_ref[...], g_ref[...]).astype(BF16)
    for t, out_ref in enumerate((q_ref, k_ref, v_ref)):
        for j in range(A_QK // wd):
            cs = slice(j * wd, (j + 1) * wd)
            ws = slice(t * A_QK + j * wd, t * A_QK + (j + 1) * wd)
            y = _dot(h, w_ref[:, ws])
            act = _conv_silu(y, hist_ref, carry_ref[t, :, cs], cw_ref[:, ws])
            carry_ref[t, :, cs] = y[tm - 8:tm, :]
            for hd in range(wd // A_DIM):
                hs = slice(hd * A_DIM, (hd + 1) * A_DIM)
                piece = act[:, hs]
                if t == 0:
                    piece = _l2(piece) * (A_DIM ** -0.5)
                elif t == 1:
                    piece = _l2(piece)
                out_ref[:, j * wd + hd * A_DIM:j * wd + (hd + 1) * A_DIM] = piece
    for j in range(A_QK // wd):
        cs = slice(j * wd, (j + 1) * wd)
        z = _dot(h, w_ref[:, 3 * A_QK + j * wd:3 * A_QK + (j + 1) * wd])
        z_ref[:, cs] = z * _sigmoid(z)
    gate_ref[...] = _dot(h, w_ref[:, 4 * A_QK:])


def _a_projection(x, seq, gain, w_in, conv_w):
    m, d = x.shape
    tm = PROJ_TM
    rows = pl.BlockSpec((tm, A_QK), lambda i: (i, 0))
    act = jax.ShapeDtypeStruct((m, A_QK), F32)
    return pl.pallas_call(
        functools.partial(_aproj_kernel, sblk=seq // tm),
        grid=(m // tm,),
        in_specs=[pl.BlockSpec((tm, d), lambda i: (i, 0)),
                  pl.BlockSpec((1, d), lambda i: (0, 0)),
                  pl.BlockSpec(w_in.shape, lambda i: (0, 0)),
                  pl.BlockSpec(conv_w.shape, lambda i: (0, 0))],
        out_specs=[rows, rows, rows, rows, pl.BlockSpec((tm, LANES), lambda i: (i, 0))],
        out_shape=[act, act, act, act, jax.ShapeDtypeStruct((m, LANES), F32)],
        scratch_shapes=[pltpu.VMEM((3, 8, A_QK), F32),
                        pltpu.VMEM((tm + 8, A_PROJ_TN), F32)],
        compiler_params=pltpu.CompilerParams(
            dimension_semantics=("arbitrary",), vmem_limit_bytes=VMEM_LIMIT),
        name="gdn_projection",
    )(x, gain.reshape(1, d), w_in, conv_w)


def _rope(y, c, s_lo, s_hi):
    return (y * c + pltpu.roll(y, LANES - ROPE_HALF, axis=1) * s_lo
            + pltpu.roll(y, ROPE_HALF, axis=1) * s_hi)


def _store_residue_major(y, n_rope, tabs, dilation, out_ref, scr_ref):
    tm, w = y.shape
    for b in range(w // LANES):
        sl = slice(b * LANES, (b + 1) * LANES)
        blk = _rope(y[:, sl], *tabs) if b < n_rope else y[:, sl]
        if dilation == 1:
            out_ref[0, 0, :, sl] = blk.astype(out_ref.dtype)
        else:
            scr_ref[b] = blk
    if dilation > 1:
        rows = tm // dilation
        for r in range(dilation):
            for b in range(w // LANES):
                out_ref[0, r, :, b * LANES:(b + 1) * LANES] = scr_ref[
                    b, pl.ds(r, rows, stride=dilation), :].astype(out_ref.dtype)


def _bproj_kernel(*refs, with_kv):
    if with_kv:
        (x_ref, gq_ref, wq_ref, gkv_ref, wkv_ref, c_ref, slo_ref, shi_ref,
         q0_ref, q1_ref, q2_ref, kv0_ref, kv1_ref, kv2_ref, scr_ref) = refs
    else:
        (x_ref, gq_ref, wq_ref, c_ref, slo_ref, shi_ref,
         q0_ref, q1_ref, q2_ref, scr_ref) = refs
    x = x_ref[...]
    xn = x * lax.rsqrt(jnp.mean(x * x, axis=-1, keepdims=True) + NORM_EPS)
    tabs = (c_ref[...], slo_ref[...], shi_ref[...])
    hq = (xn * gq_ref[...]).astype(BF16)
    for g, (q_ref, (_, dil)) in enumerate(zip((q0_ref, q1_ref, q2_ref), B_GROUPS)):
        y = _dot(hq, wq_ref[:, g * B_OUT:(g + 1) * B_OUT])
        _store_residue_major(y, B_HEADS, tabs, dil, q_ref, scr_ref)
    if with_kv:
        hkv = (xn * gkv_ref[...]).astype(BF16)
        for g, (kv_ref, (_, dil)) in enumerate(zip((kv0_ref, kv1_ref, kv2_ref), B_GROUPS)):
            y = _dot(hkv, wkv_ref[:, g * 2 * B_KW:(g + 1) * 2 * B_KW])
            _store_residue_major(y, B_KV_HEADS, tabs, dil, kv_ref, scr_ref)


def _b_projection(x, bsz, seq, rope, g_q, w_q, g_kv=None, w_kv=None):
    m, d = x.shape
    tm = PROJ_TM
    sblk = seq // tm
    with_kv = w_kv is not None

    def whole(a):
        return pl.BlockSpec(a.shape, lambda i: (0, 0))

    def out(width, dil):
        return pl.BlockSpec((1, dil, tm // dil, width), lambda i: (i // sblk, 0, i % sblk, 0))

    def out_shape(width, dil):
        return jax.ShapeDtypeStruct((bsz, dil, seq // dil, width), BF16)

    tab = pl.BlockSpec((tm, LANES), lambda i: (i % sblk, 0))
    args = [x, g_q.reshape(1, d), w_q]
    in_specs = [pl.BlockSpec((tm, d), lambda i: (i, 0)), whole(args[1]), whole(w_q)]
    if with_kv:
        args += [g_kv.reshape(1, d), w_kv]
        in_specs += [whole(args[3]), whole(w_kv)]
    args += list(rope)
    in_specs += [tab, tab, tab]
    widths = [B_OUT] * B_N_GROUPS + ([2 * B_KW] * B_N_GROUPS if with_kv else [])
    dils = [dil for _, dil in B_GROUPS] * (2 if with_kv else 1)
    return pl.pallas_call(
        functools.partial(_bproj_kernel, with_kv=with_kv),
        grid=(m // tm,),
        in_specs=in_specs,
        out_specs=[out(w, dl) for w, dl in zip(widths, dils)],
        out_shape=[out_shape(w, dl) for w, dl in zip(widths, dils)],
        scratch_shapes=[pltpu.VMEM((B_HEADS, tm, LANES), F32)],
        compiler_params=pltpu.CompilerParams(
            dimension_semantics=("parallel",), vmem_limit_bytes=VMEM_LIMIT),
        name="attn_projection",
    )(*args)


_G_BETA, _G_CUM, _G_EXP, _G_KDEC, _G_LAST = (i * A_HEADS for i in range(5))


def _gdn_kernel(q_ref, k_ref, v_ref, z_ref, gate_ref, alog_ref, dt_ref, onorm_ref,
                o_ref, state_ref, gt_ref, gs_ref, lhs_ref, u_ref, attn_ref, kdt_ref, vnew_ref):
    ts = q_ref.shape[1]
    c = A_CHUNK
    nc = ts // c
    nh = A_HEADS

    @pl.when(pl.program_id(1) == 0)
    def _():
        state_ref[...] = jnp.zeros_like(state_ref)

    gates_t = gate_ref[0].T
    a = gates_t[nh:2 * nh] + dt_ref[...]
    softplus = jnp.maximum(a, 0.0) + jnp.log(1.0 + jnp.exp(-jnp.abs(a)))
    g = -jnp.exp(alog_ref[...]) * softplus
    lane_in_chunk = lax.broadcasted_iota(jnp.int32, (nh, ts), 1) & (c - 1)
    s = 1
    while s < c:
        g = g + jnp.where(lane_in_chunk >= s, pltpu.roll(g, s, axis=1), 0.0)
        s *= 2
    g_last = jnp.concatenate(
        [jnp.broadcast_to(g[:, (n + 1) * c - 1:(n + 1) * c], (nh, c)) for n in range(nc)], axis=1)
    rows = jnp.concatenate(
        [_sigmoid(gates_t[0:nh]), g, jnp.exp(g), jnp.exp(g_last - g), jnp.exp(g_last),
         jnp.zeros((LANES - 5 * nh, ts), F32)], axis=0)
    gt_ref[...] = g
    gs_ref[...] = rows.T
    vnew_ref[...] = jnp.zeros_like(vnew_ref)

    ri = lax.broadcasted_iota(jnp.int32, (c, ts), 0)
    ci = lax.broadcasted_iota(jnp.int32, (c, ts), 1)
    incl = ri >= (ci & (c - 1))
    strict = ri > (ci & (c - 1))
    eye = jnp.where(ri == (ci & (c - 1)), 1.0, 0.0)
    chunk_of_lane = ci // c
    cat_mask = [jnp.where(chunk_of_lane == n, 1.0, 0.0).astype(BF16) for n in range(nc)]
    wide_lane = lax.broadcasted_iota(jnp.int32, (A_DIM, ts), 1) // c
    wide_mask = [jnp.where(wide_lane == n, 1.0, 0.0).astype(BF16) for n in range(nc)]

    def block_diag(cat):
        return jnp.concatenate([cat * cat_mask[n] for n in range(nc)], axis=0)

    inv, power = {}, {}
    r_all, qdec = {}, {}
    for h in range(nh):
        hs = slice(h * A_DIM, (h + 1) * A_DIM)
        q, k, v = q_ref[0, :, hs], k_ref[0, :, hs], v_ref[0, :, hs]
        bc = gs_ref[:, _G_BETA + h:_G_BETA + h + 1]
        gc = gs_ref[:, _G_CUM + h:_G_CUM + h + 1]
        egc = gs_ref[:, _G_EXP + h:_G_EXP + h + 1]
        kdf = gs_ref[:, _G_KDEC + h:_G_KDEC + h + 1]
        g_row = gt_ref[h:h + 1, :]
        kb = k * bc
        kt = k.T.astype(BF16)
        k_rhs = jnp.concatenate([kt * wide_mask[n] for n in range(nc)], axis=0)
        lhs = jnp.concatenate(
            [jnp.concatenate([kb[n * c:(n + 1) * c], q[n * c:(n + 1) * c]], axis=0)
             for n in range(nc)], axis=1).astype(BF16)
        qk = _dot(lhs, k_rhs)
        gc_cat = jnp.broadcast_to(gc[(nc - 1) * c:nc * c], (c, ts))
        for n in range(nc - 2, -1, -1):
            gc_cat = jnp.where(chunk_of_lane == n,
                               jnp.broadcast_to(gc[n * c:(n + 1) * c], (c, ts)), gc_cat)
        decay = jnp.exp(jnp.where(incl, gc_cat - g_row, -jnp.inf))
        lower = jnp.where(strict, qk[:c] * decay, 0.0)
        attn = jnp.where(incl, qk[c:] * decay, 0.0).astype(BF16)
        kdt = (k * kdf).T.astype(BF16)
        for n in range(nc):
            pair = slice((n // 2) * 2 * c, (n // 2 + 1) * 2 * c)
            attn_ref[h, n] = (attn * cat_mask[n])[:, pair]
            kdt_ref[h, n] = (kdt * wide_mask[n])[:, pair]
        r_all[h] = jnp.concatenate([v * bc, kb * egc], axis=1).astype(BF16)
        qdec[h] = (q * egc).astype(BF16)
        inv[h] = eye - lower
        power[h] = lower.astype(BF16)
    diag = {h: block_diag(power[h]) for h in range(nh)}
    p = 2
    while p < c:
        for h in range(nh):
            power[h] = _dot(power[h], diag[h]).astype(BF16)
            diag[h] = block_diag(power[h])
            inv[h] = inv[h] + _dot(inv[h].astype(BF16), diag[h])
        p *= 2
    for h in range(nh):
        uw = _dot(block_diag(inv[h].astype(BF16)), r_all[h])
        u_ref[h] = uw[:, :A_DIM]
        w = uw[:, A_DIM:].astype(BF16)
        for n in range(nc):
            sl = slice(n * c, (n + 1) * c)
            lhs_ref[h, n] = jnp.concatenate([w[sl], qdec[h][sl]], axis=0)

    onorm = onorm_ref[...]
    for n in range(nc):
        sl = slice(n * c, (n + 1) * c)
        pair = slice((n // 2) * 2 * c, (n // 2 + 1) * 2 * c)
        ws = [_dot(lhs_ref[h, n], state_ref[h].astype(BF16)) for h in range(nh)]
        for h in range(nh):
            hs = slice(h * A_DIM, (h + 1) * A_DIM)
            vnew_ref[h, sl, :] = (u_ref[h, sl, :] - ws[h][:c]).astype(BF16)
            v_pair = vnew_ref[h, pair, :]
            o = ws[h][c:] + _dot(attn_ref[h, n], v_pair)
            chunk_decay = gs_ref[n * c:n * c + 1, _G_LAST + h:_G_LAST + h + 1]
            state_ref[h] = state_ref[h] * chunk_decay + _dot(kdt_ref[h, n], v_pair)
            o = o * lax.rsqrt(jnp.mean(o * o, axis=-1, keepdims=True) + NORM_EPS)
            o_ref[0, sl, hs] = (o * onorm * z_ref[0, sl, hs]).astype(o_ref.dtype)


def _gated_deltanet(q, k, v, z, gates, a_log, dt_bias, out_norm):
    bsz, seq, _ = q.shape
    ts = GDN_TS
    nh = A_HEADS
    nc = ts // A_CHUNK
    act = pl.BlockSpec((1, ts, A_QK), lambda b, s: (b, s, 0))
    col = pl.BlockSpec((nh, 1), lambda b, s: (0, 0))
    return pl.pallas_call(
        _gdn_kernel,
        grid=(bsz, seq // ts),
        in_specs=[act, act, act, act, pl.BlockSpec((1, ts, LANES), lambda b, s: (b, s, 0)),
                  col, col, pl.BlockSpec((1, LANES), lambda b, s: (0, 0))],
        out_specs=act,
        out_shape=jax.ShapeDtypeStruct((bsz, seq, A_QK), BF16),
        scratch_shapes=[pltpu.VMEM((nh, A_DIM, A_DIM), F32),
                        pltpu.VMEM((nh, ts), F32),
                        pltpu.VMEM((ts, LANES), F32),
                        pltpu.VMEM((nh, nc, 2 * A_CHUNK, A_DIM), BF16),
                        pltpu.VMEM((nh, ts, A_DIM), F32),
                        pltpu.VMEM((nh, nc, A_CHUNK, 2 * A_CHUNK), BF16),
                        pltpu.VMEM((nh, nc, A_DIM, 2 * A_CHUNK), BF16),
                        pltpu.VMEM((nh, ts, A_DIM), BF16)],
        compiler_params=pltpu.CompilerParams(
            dimension_semantics=("parallel", "arbitrary"), vmem_limit_bytes=VMEM_LIMIT),
        name="gated_delta_rule",
    )(q, k, v, z, gates, a_log.reshape(nh, 1), dt_bias.reshape(nh, 1),
      out_norm.reshape(1, LANES))


def _attn_kernel(q_ref, kvc_ref, kvp_ref, bias_ref, o_ref, lse_ref, *, nq):
    step = pl.program_id(2)
    blk = B_BLK
    dh = B_HEAD_DIM
    scale = dh ** -0.5
    kv_all = jnp.concatenate([kvp_ref[0, 0], kvc_ref[0, 0]], axis=0)
    lane = lax.broadcasted_iota(jnp.int32, (blk, LANES), 1)
    for i in range(nq):
        rows = slice(i * blk, (i + 1) * blk)
        keys = kv_all[i * blk:(i + 2) * blk]
        bias = bias_ref[jnp.minimum(step, 1)] if i == 0 else bias_ref[1]
        lse_tile = jnp.zeros((blk, LANES), F32)
        for g in range(B_KV_HEADS):
            qs = jnp.concatenate(
                [q_ref[0, 0, rows, (g * B_REP + r) * dh:(g * B_REP + r + 1) * dh]
                 for r in range(B_REP)], axis=0)
            s = _dot_nt(qs, keys[:, g * dh:(g + 1) * dh]) + bias
            m = jnp.max(s, axis=-1, keepdims=True)
            e = jnp.exp2((s - m) * (scale * LOG2E))
            den = jnp.sum(e, axis=-1, keepdims=True)
            o = _dot(e.astype(BF16), keys[:, B_KW + g * dh:B_KW + (g + 1) * dh]) / den
            lse = m * scale + jnp.log(den)
            for r in range(B_REP):
                hd = g * B_REP + r
                o_ref[0, 0, rows, hd * dh:(hd + 1) * dh] = o[r * blk:(r + 1) * blk]
                lse_tile = jnp.where(lane == hd, lse[r * blk:(r + 1) * blk], lse_tile)
        lse_ref[0, 0, rows, :] = lse_tile


def _band_bias():
    qi = jnp.arange(B_REP * B_BLK)[:, None] & (B_BLK - 1)
    kj = jnp.arange(2 * B_BLK)[None, :]
    dist = qi + B_BLK - kj
    band = (dist >= 0) & (dist <= B_BLK)
    first = band & (kj >= B_BLK)
    return jnp.where(jnp.stack([first, band]), 0.0, -jnp.inf).astype(F32)


def _dilated_attention(q, kv, bias, gi):
    bsz, dil, length, _ = q.shape
    nq = min(ATTN_QBLOCKS, length // B_BLK)
    tq = nq * B_BLK
    return pl.pallas_call(
        functools.partial(_attn_kernel, nq=nq),
        grid=(bsz, dil, length // tq),
        in_specs=[pl.BlockSpec((1, 1, tq, B_OUT), lambda b, r, n: (b, r, n, 0)),
                  pl.BlockSpec((1, 1, tq, 2 * B_KW), lambda b, r, n: (b, r, n, 0)),
                  pl.BlockSpec((1, 1, B_BLK, 2 * B_KW),
                               lambda b, r, n: (b, r, jnp.maximum(nq * n - 1, 0), 0)),
                  pl.BlockSpec(bias.shape, lambda b, r, n: (0, 0, 0))],
        out_specs=[pl.BlockSpec((1, 1, tq, B_OUT), lambda b, r, n: (b, r, n, 0)),
                   pl.BlockSpec((1, 1, tq, LANES), lambda b, r, n: (b, r, n, 0))],
        out_shape=[jax.ShapeDtypeStruct((bsz, dil, length, B_OUT), F32),
                   jax.ShapeDtypeStruct((bsz, dil, length, LANES), F32)],
        compiler_params=pltpu.CompilerParams(
            dimension_semantics=("parallel", "parallel", "arbitrary"),
            vmem_limit_bytes=VMEM_LIMIT),
        name=f"dilated_attention_g{gi}",
    )(q, kv, kv, bias)


def _merge_groups(o_refs, lse_refs, wts_ref, mixed_ref):
    tm = mixed_ref.shape[1]
    dils = [r.shape[1] for r in o_refs]
    for g, dil in enumerate(dils):
        rows = tm // dil
        for r in range(dil):
            wts_ref[g, pl.ds(r, rows, stride=dil), :] = lse_refs[g][0, r]
    lses = [wts_ref[g] for g in range(len(dils))]
    mx = functools.reduce(jnp.maximum, lses)
    wts = [jnp.exp(l - mx) for l in lses]
    inv = 1.0 / functools.reduce(jnp.add, wts)
    for g in range(len(dils)):
        wts_ref[g] = wts[g] * inv
    for g, dil in enumerate(dils):
        rows = tm // dil
        for r in range(dil):
            idx = pl.ds(r, rows, stride=dil)
            w = wts_ref[g, idx, :]
            for hd in range(B_HEADS):
                sl = slice(hd * B_HEAD_DIM, (hd + 1) * B_HEAD_DIM)
                term = w[:, hd:hd + 1] * o_refs[g][0, r, :, sl]
                if g == 0:
                    mixed_ref[hd, idx, :] = term
                else:
                    mixed_ref[hd, idx, :] += term


def _tail_kernel(*refs, n_mix, final):
    x_ref = refs[0]
    o_refs = refs[1:1 + n_mix]
    n_lse = n_mix if n_mix > 1 else 0
    lse_refs = refs[1 + n_mix:1 + n_mix + n_lse]
    rest = refs[1 + n_mix + n_lse:]
    p_ref, wo_ref, gm_ref, wup_ref, wdn_ref, wg_ref, wp_ref, gf_ref, out_ref = rest[:9]
    if n_mix > 1:
        wts_ref, mixed_ref = rest[9:]
        _merge_groups(o_refs, lse_refs, wts_ref, mixed_ref)
        mixed = jnp.concatenate([mixed_ref[hd].astype(BF16) for hd in range(B_HEADS)], axis=1)
    else:
        mixed = o_refs[0][...]
    x1 = x_ref[...] + _dot(mixed, wo_ref[...])
    h = _rms(x1, gm_ref[...]).astype(BF16)
    acc = x1
    for f in range(D_FF // TAIL_TF):
        fs = slice(f * TAIL_TF, (f + 1) * TAIL_TF)
        u = jnp.maximum(_dot(h, wup_ref[:, fs]), 0.0)
        acc = acc + _dot((u * u).astype(BF16), wdn_ref[fs, :])
    gate = _sigmoid(_dot(acc.astype(BF16), wg_ref[...]))
    x3 = acc + gate * _dot(p_ref[...].astype(BF16), wp_ref[...])
    if final:
        x3 = _rms(x3, gf_ref[...])
    out_ref[...] = x3


def _layer_tail(x, seq, mix, lses, p, w_o, g_mlp, w_up, w_dn, w_gate, w_ple, g_final, *, final):
    m, d = x.shape
    n_mix = len(mix)
    tm = TAIL_TM if n_mix == 1 else TAIL_TM_MERGE
    sblk = seq // tm

    def rows(width):
        return pl.BlockSpec((tm, width), lambda i: (i, 0))

    def resident(a):
        return pl.BlockSpec(a.shape, lambda i: (0, 0), pipeline_mode=pl.Buffered(1))

    def residue_major(a):
        dil, width = a.shape[1], a.shape[3]
        return pl.BlockSpec((1, dil, tm // dil, width), lambda i: (i // sblk, 0, i % sblk, 0))

    g_mlp = g_mlp.reshape(1, d)
    g_final = g_final.reshape(1, d)
    if n_mix > 1:
        mix_specs = [residue_major(a) for a in list(mix) + list(lses)]
        scratch = [pltpu.VMEM((n_mix, tm, LANES), F32),
                   pltpu.VMEM((B_HEADS, tm, B_HEAD_DIM), F32)]
    else:
        mix_specs = [rows(d)]
        scratch = []
    weights = (w_o, g_mlp, w_up, w_dn, w_gate, w_ple, g_final)
    return pl.pallas_call(
        functools.partial(_tail_kernel, n_mix=n_mix, final=final),
        grid=(m // tm,),
        in_specs=[rows(d)] + mix_specs + [rows(PLE_DIM)] + [resident(w) for w in weights],
        out_specs=rows(d),
        out_shape=jax.ShapeDtypeStruct((m, d), F32),
        scratch_shapes=scratch,
        compiler_params=pltpu.CompilerParams(
            dimension_semantics=("parallel",), vmem_limit_bytes=VMEM_LIMIT),
        name="layer_tail",
    )(x, *mix, *lses, p, *weights)


def _rope_tables(seq):
    inv = jnp.power(jnp.float32(ROPE_THETA), -jnp.arange(0, ROPE_DIM, 2, dtype=F32) / ROPE_DIM)
    ang = jnp.arange(seq, dtype=F32)[:, None] * inv[None, :]
    cos, sin = jnp.cos(ang), jnp.sin(ang)
    zeros = jnp.zeros((seq, LANES - ROPE_DIM), F32)
    zhalf = jnp.zeros((seq, ROPE_HALF), F32)
    c = jnp.concatenate([cos, cos, jnp.ones_like(zeros)], axis=1)
    s_lo = jnp.concatenate([-sin, zhalf, zeros], axis=1)
    s_hi = jnp.concatenate([zhalf, sin, zeros], axis=1)
    return c, s_lo, s_hi


def _group_kv_columns(w_kv):
    d = w_kv.shape[0]
    w = w_kv.reshape(d, 2, B_N_GROUPS, B_KW)
    return jnp.transpose(w, (0, 2, 1, 3)).reshape(d, 2 * B_N_GROUPS * B_KW)


def kernel(x, p, attn_norm, mlp_norm, a_w_in, a_conv_w, a_log, a_dt_bias, a_out_norm, a_w_out,
           kv_norm, b_w_kv, b_w_q, b_w_o, mlp_w_up, mlp_w_down, ple_w_proj, ple_w_gate, final_norm):
    bsz, seq, d = x.shape
    m = bsz * seq
    xf = x.reshape(m, d)
    rope = _rope_tables(seq)
    bias = _band_bias()
    kvs = None
    for i in range(DEPTH):
        if i < N_A_LAYERS:
            w_in = jnp.pad(a_w_in[i], ((0, 0), (0, A_IN_PAD - A_IN_COLS))).astype(BF16)
            q, k, v, z, gates = _a_projection(xf, seq, attn_norm[i], w_in, a_conv_w[i])
            o = _gated_deltanet(*(t.reshape(bsz, seq, -1) for t in (q, k, v, z, gates)),
                                a_log[i], a_dt_bias[i], a_out_norm[i])
            mix, lses, w_o = [o.reshape(m, A_QK)], [], a_w_out[i]
        else:
            j = i - N_A_LAYERS
            w_q = b_w_q[j].astype(BF16)
            if kvs is None:
                outs = _b_projection(xf, bsz, seq, rope, attn_norm[i], w_q, kv_norm,
                                     _group_kv_columns(b_w_kv).astype(BF16))
                qs, kvs = outs[:B_N_GROUPS], outs[B_N_GROUPS:]
            else:
                qs = _b_projection(xf, bsz, seq, rope, attn_norm[i], w_q)
            mix, lses = [], []
            for gi in range(B_N_GROUPS):
                o_g, lse_g = _dilated_attention(qs[gi], kvs[gi], bias, gi)
                mix.append(o_g)
                lses.append(lse_g)
            w_o = b_w_o[j]
        xf = _layer_tail(xf, seq, mix, lses, p[i].reshape(m, PLE_DIM), w_o.astype(BF16),
                         mlp_norm[i], mlp_w_up[i].astype(BF16), mlp_w_down[i].astype(BF16),
                         ple_w_gate[i].astype(BF16), ple_w_proj[i].astype(BF16), final_norm,
                         final=(i == DEPTH - 1))
    return xf.reshape(bsz, seq, d)
```

```python
import functools

import jax
import jax.numpy as jnp
from jax import lax
from jax.experimental import pallas as pl
from jax.experimental.pallas import tpu as pltpu

F32 = jnp.float32
BF16 = jnp.bfloat16

D_MODEL = 1024
DEPTH = 4
N_A_LAYERS = DEPTH // 2
PLE_DIM = 256
D_FF = 4 * D_MODEL
NORM_EPS = 1e-6

A_HEADS = 8
A_DIM = 128
A_CONV = 4
A_CHUNK = 64
A_QK = A_HEADS * A_DIM
A_IN_COLS = 4 * A_QK + 2 * A_HEADS
LANES = 128
A_IN_PAD = 4 * A_QK + LANES

B_HEAD_DIM = 128
B_GROUPS = ((128, 1), (512, 4), (2048, 16))
B_N_GROUPS = len(B_GROUPS)
B_HEADS = 8
B_KV_HEADS = 2
B_REP = B_HEADS // B_KV_HEADS
B_BLK = 128
B_OUT = B_HEADS * B_HEAD_DIM
B_KW = B_KV_HEADS * B_HEAD_DIM
ROPE_THETA = 500000.0
ROPE_DIM = B_HEAD_DIM // 4
ROPE_HALF = ROPE_DIM // 2

VMEM_LIMIT = 56 * 1024 * 1024

PROJ_TM = 512
A_PROJ_TN = 256
GDN_TS = 256
ATTN_TQ = 512
ATTN_C2 = (B_HEAD_DIM ** -0.5) * 1.4426950408889634
TAIL_TM = 512
TAIL_TF = 512


def _dot(a, b):
    return jnp.dot(a, b, preferred_element_type=F32)


def _dot_nt(a, b):
    return lax.dot_general(a, b, (((1,), (1,)), ((), ())), preferred_element_type=F32)


def _rms(x, gain):
    return x * lax.rsqrt(jnp.mean(x * x, axis=-1, keepdims=True) + NORM_EPS) * gain


def _sigmoid(x):
    return 1.0 / (1.0 + jnp.exp(-x))


def _conv_silu(y, hist_ref, prev8, w):
    tm = y.shape[0]
    hist_ref[0:8, :] = prev8
    hist_ref[8:8 + tm, :] = y
    acc = y * w[A_CONV - 1:A_CONV, :]
    for s in range(1, A_CONV):
        acc = acc + hist_ref[8 - s:8 - s + tm, :] * w[A_CONV - 1 - s:A_CONV - s, :]
    return acc * _sigmoid(acc)


def _l2(x):
    return x * lax.rsqrt(jnp.sum(x * x, axis=-1, keepdims=True) + NORM_EPS)


def _aproj_kernel(x_ref, g_ref, w_ref, cw_ref, q_ref, k_ref, v_ref, z_ref, gate_ref,
                  carry_ref, hist_ref, *, sblk):
    tm = x_ref.shape[0]
    wd = A_PROJ_TN

    @pl.when(pl.program_id(0) % sblk == 0)
    def _():
        carry_ref[...] = jnp.zeros_like(carry_ref)

    h = _rms(x_ref[...], g_ref[...]).astype(BF16)
    for t, out_ref in enumerate((q_ref, k_ref, v_ref)):
        for j in range(A_QK // wd):
            cs = slice(j * wd, (j + 1) * wd)
            ws = slice(t * A_QK + j * wd, t * A_QK + (j + 1) * wd)
            y = _dot(h, w_ref[:, ws])
            act = _conv_silu(y, hist_ref, carry_ref[t, :, cs], cw_ref[:, ws])
            carry_ref[t, :, cs] = y[tm - 8:tm, :]
            for hd in range(wd // A_DIM):
                hs = slice(hd * A_DIM, (hd + 1) * A_DIM)
                piece = act[:, hs]
                if t == 0:
                    piece = _l2(piece) * (A_DIM ** -0.5)
                elif t == 1:
                    piece = _l2(piece)
                out_ref[:, j * wd + hd * A_DIM:j * wd + (hd + 1) * A_DIM] = piece
    for j in range(A_QK // wd):
        cs = slice(j * wd, (j + 1) * wd)
        z = _dot(h, w_ref[:, 3 * A_QK + j * wd:3 * A_QK + (j + 1) * wd])
        z_ref[:, cs] = z * _sigmoid(z)
    gate_ref[...] = _dot(h, w_ref[:, 4 * A_QK:])


def _a_projection(x, seq, gain, w_in, conv_w):
    m, d = x.shape
    tm = PROJ_TM
    rows = pl.BlockSpec((tm, A_QK), lambda i: (i, 0))
    act = jax.ShapeDtypeStruct((m, A_QK), F32)
    return pl.pallas_call(
        functools.partial(_aproj_kernel, sblk=seq // tm),
        grid=(m // tm,),
        in_specs=[pl.BlockSpec((tm, d), lambda i: (i, 0)),
                  pl.BlockSpec((1, d), lambda i: (0, 0)),
                  pl.BlockSpec(w_in.shape, lambda i: (0, 0)),
                  pl.BlockSpec(conv_w.shape, lambda i: (0, 0))],
        out_specs=[rows, rows, rows, rows, pl.BlockSpec((tm, LANES), lambda i: (i, 0))],
        out_shape=[act, act, act, act, jax.ShapeDtypeStruct((m, LANES), F32)],
        scratch_shapes=[pltpu.VMEM((3, 8, A_QK), F32),
                        pltpu.VMEM((tm + 8, A_PROJ_TN), F32)],
        compiler_params=pltpu.CompilerParams(
            dimension_semantics=("arbitrary",), vmem_limit_bytes=VMEM_LIMIT),
        name="gdn_projection",
    )(x, gain.reshape(1, d), w_in, conv_w)


def _rope(y, c, s_lo, s_hi):
    return (y * c + pltpu.roll(y, LANES - ROPE_HALF, axis=1) * s_lo
            + pltpu.roll(y, ROPE_HALF, axis=1) * s_hi)


def _store_residue_major(y, n_rope, tabs, dilation, out_ref, scr_ref):
    tm, w = y.shape
    for b in range(w // LANES):
        sl = slice(b * LANES, (b + 1) * LANES)
        blk = _rope(y[:, sl], *tabs) if b < n_rope else y[:, sl]
        if dilation == 1:
            out_ref[0, 0, :, sl] = blk.astype(out_ref.dtype)
        else:
            scr_ref[b] = blk
    if dilation > 1:
        rows = tm // dilation
        for r in range(dilation):
            for b in range(w // LANES):
                out_ref[0, r, :, b * LANES:(b + 1) * LANES] = scr_ref[
                    b, pl.ds(r, rows, stride=dilation), :].astype(out_ref.dtype)


def _bproj_kernel(*refs, with_kv):
    if with_kv:
        (x_ref, gq_ref, wq_ref, gkv_ref, wkv_ref, c_ref, slo_ref, shi_ref,
         q0_ref, q1_ref, q2_ref, kv0_ref, kv1_ref, kv2_ref, scr_ref) = refs
    else:
        (x_ref, gq_ref, wq_ref, c_ref, slo_ref, shi_ref,
         q0_ref, q1_ref, q2_ref, scr_ref) = refs
    x = x_ref[...]
    xn = x * lax.rsqrt(jnp.mean(x * x, axis=-1, keepdims=True) + NORM_EPS)
    tabs = (c_ref[...], slo_ref[...], shi_ref[...])
    hq = (xn * gq_ref[...]).astype(BF16)
    for g, (q_ref, (_, dil)) in enumerate(zip((q0_ref, q1_ref, q2_ref), B_GROUPS)):
        y = _dot(hq, wq_ref[:, g * B_OUT:(g + 1) * B_OUT])
        _store_residue_major(y, B_HEADS, tabs, dil, q_ref, scr_ref)
    if with_kv:
        hkv = (xn * gkv_ref[...]).astype(BF16)
        for g, (kv_ref, (_, dil)) in enumerate(zip((kv0_ref, kv1_ref, kv2_ref), B_GROUPS)):
            y = _dot(hkv, wkv_ref[:, g * 2 * B_KW:(g + 1) * 2 * B_KW])
            _store_residue_major(y, B_KV_HEADS, tabs, dil, kv_ref, scr_ref)


def _b_projection(x, bsz, seq, rope, g_q, w_q, g_kv=None, w_kv=None):
    m, d = x.shape
    tm = PROJ_TM
    sblk = seq // tm
    with_kv = w_kv is not None

    def whole(a):
        return pl.BlockSpec(a.shape, lambda i: (0, 0))

    def out(width, dil):
        return pl.BlockSpec((1, dil, tm // dil, width), lambda i: (i // sblk, 0, i % sblk, 0))

    def out_shape(width, dil):
        return jax.ShapeDtypeStruct((bsz, dil, seq // dil, width), BF16)

    tab = pl.BlockSpec((tm, LANES), lambda i: (i % sblk, 0))
    args = [x, g_q.reshape(1, d), w_q]
    in_specs = [pl.BlockSpec((tm, d), lambda i: (i, 0)), whole(args[1]), whole(w_q)]
    if with_kv:
        args += [g_kv.reshape(1, d), w_kv]
        in_specs += [whole(args[3]), whole(w_kv)]
    args += list(rope)
    in_specs += [tab, tab, tab]
    widths = [B_OUT] * B_N_GROUPS + ([2 * B_KW] * B_N_GROUPS if with_kv else [])
    dils = [dil for _, dil in B_GROUPS] * (2 if with_kv else 1)
    return pl.pallas_call(
        functools.partial(_bproj_kernel, with_kv=with_kv),
        grid=(m // tm,),
        in_specs=in_specs,
        out_specs=[out(w, dl) for w, dl in zip(widths, dils)],
        out_shape=[out_shape(w, dl) for w, dl in zip(widths, dils)],
        scratch_shapes=[pltpu.VMEM((B_HEADS, tm, LANES), F32)],
        compiler_params=pltpu.CompilerParams(
            dimension_semantics=("parallel",), vmem_limit_bytes=VMEM_LIMIT),
        name="attn_projection",
    )(*args)


_G_BETA, _G_CUM, _G_EXP, _G_KDEC, _G_LAST = (i * A_HEADS for i in range(5))


def _gdn_kernel(q_ref, k_ref, v_ref, z_ref, gate_ref, alog_ref, dt_ref, onorm_ref,
                o_ref, state_ref, gt_ref, gs_ref, lhs_ref, u_ref, attn_ref, kdt_ref, vnew_ref):
    ts = q_ref.shape[1]
    c = A_CHUNK
    nc = ts // c
    nh = A_HEADS

    @pl.when(pl.program_id(1) == 0)
    def _():
        state_ref[...] = jnp.zeros_like(state_ref)

    gates_t = gate_ref[0].T
    a = gates_t[nh:2 * nh] + dt_ref[...]
    softplus = jnp.maximum(a, 0.0) + jnp.log(1.0 + jnp.exp(-jnp.abs(a)))
    g = -jnp.exp(alog_ref[...]) * softplus
    lane_in_chunk = lax.broadcasted_iota(jnp.int32, (nh, ts), 1) & (c - 1)
    s = 1
    while s < c:
        g = g + jnp.where(lane_in_chunk >= s, pltpu.roll(g, s, axis=1), 0.0)
        s *= 2
    g_last = jnp.concatenate(
        [jnp.broadcast_to(g[:, (n + 1) * c - 1:(n + 1) * c], (nh, c)) for n in range(nc)], axis=1)
    rows = jnp.concatenate(
        [_sigmoid(gates_t[0:nh]), g, jnp.exp(g), jnp.exp(g_last - g), jnp.exp(g_last),
         jnp.zeros((LANES - 5 * nh, ts), F32)], axis=0)
    gt_ref[...] = g
    gs_ref[...] = rows.T
    vnew_ref[...] = jnp.zeros_like(vnew_ref)

    ri = lax.broadcasted_iota(jnp.int32, (c, ts), 0)
    ci = lax.broadcasted_iota(jnp.int32, (c, ts), 1)
    incl = ri >= (ci & (c - 1))
    strict = ri > (ci & (c - 1))
    eye = jnp.where(ri == (ci & (c - 1)), 1.0, 0.0)
    chunk_of_lane = ci // c
    cat_mask = [jnp.where(chunk_of_lane == n, 1.0, 0.0).astype(BF16) for n in range(nc)]
    wide_lane = lax.broadcasted_iota(jnp.int32, (A_DIM, ts), 1) // c
    wide_mask = [jnp.where(wide_lane == n, 1.0, 0.0).astype(BF16) for n in range(nc)]

    def block_diag(cat):
        return jnp.concatenate([cat * cat_mask[n] for n in range(nc)], axis=0)

    inv, power = {}, {}
    r_all, qdec = {}, {}
    for h in range(nh):
        hs = slice(h * A_DIM, (h + 1) * A_DIM)
        q, k, v = q_ref[0, :, hs], k_ref[0, :, hs], v_ref[0, :, hs]
        bc = gs_ref[:, _G_BETA + h:_G_BETA + h + 1]
        gc = gs_ref[:, _G_CUM + h:_G_CUM + h + 1]
        egc = gs_ref[:, _G_EXP + h:_G_EXP + h + 1]
        kdf = gs_ref[:, _G_KDEC + h:_G_KDEC + h + 1]
        g_row = gt_ref[h:h + 1, :]
        kb = k * bc
        kt = k.T.astype(BF16)
        k_rhs = jnp.concatenate([kt * wide_mask[n] for n in range(nc)], axis=0)
        lhs = jnp.concatenate(
            [jnp.concatenate([kb[n * c:(n + 1) * c], q[n * c:(n + 1) * c]], axis=0)
             for n in range(nc)], axis=1).astype(BF16)
        qk = _dot(lhs, k_rhs)
        gc_cat = jnp.broadcast_to(gc[(nc - 1) * c:nc * c], (c, ts))
        for n in range(nc - 2, -1, -1):
            gc_cat = jnp.where(chunk_of_lane == n,
                               jnp.broadcast_to(gc[n * c:(n + 1) * c], (c, ts)), gc_cat)
        decay = jnp.exp(jnp.where(incl, gc_cat - g_row, -jnp.inf))
        lower = jnp.where(strict, qk[:c] * decay, 0.0)
        attn = jnp.where(incl, qk[c:] * decay, 0.0).astype(BF16)
        kdt = (k * kdf).T.astype(BF16)
        for n in range(nc):
            pair = slice((n // 2) * 2 * c, (n // 2 + 1) * 2 * c)
            attn_ref[h, n] = (attn * cat_mask[n])[:, pair]
            kdt_ref[h, n] = (kdt * wide_mask[n])[:, pair]
        r_all[h] = jnp.concatenate([v * bc, kb * egc], axis=1).astype(BF16)
        qdec[h] = (q * egc).astype(BF16)
        inv[h] = eye - lower
        power[h] = lower.astype(BF16)
    diag = {h: block_diag(power[h]) for h in range(nh)}
    p = 2
    while p < c:
        for h in range(nh):
            power[h] = _dot(power[h], diag[h]).astype(BF16)
            diag[h] = block_diag(power[h])
            inv[h] = inv[h] + _dot(inv[h].astype(BF16), diag[h])
        p *= 2
    for h in range(nh):
        uw = _dot(block_diag(inv[h].astype(BF16)), r_all[h])
        u_ref[h] = uw[:, :A_DIM]
        w = uw[:, A_DIM:].astype(BF16)
        for n in range(nc):
            sl = slice(n * c, (n + 1) * c)
            lhs_ref[h, n] = jnp.concatenate([w[sl], qdec[h][sl]], axis=0)

    onorm = onorm_ref[...]
    for n in range(nc):
        sl = slice(n * c, (n + 1) * c)
        pair = slice((n // 2) * 2 * c, (n // 2 + 1) * 2 * c)
        ws = [_dot(lhs_ref[h, n], state_ref[h].astype(BF16)) for h in range(nh)]
        for h in range(nh):
            hs = slice(h * A_DIM, (h + 1) * A_DIM)
            vnew_ref[h, sl, :] = (u_ref[h, sl, :] - ws[h][:c]).astype(BF16)
            v_pair = vnew_ref[h, pair, :]
            o = ws[h][c:] + _dot(attn_ref[h, n], v_pair)
            chunk_decay = gs_ref[n * c:n * c + 1, _G_LAST + h:_G_LAST + h + 1]
            state_ref[h] = state_ref[h] * chunk_decay + _dot(kdt_ref[h, n], v_pair)
            o = o * lax.rsqrt(jnp.mean(o * o, axis=-1, keepdims=True) + NORM_EPS)
            o_ref[0, sl, hs] = (o * onorm * z_ref[0, sl, hs]).astype(o_ref.dtype)


def _gated_deltanet(q, k, v, z, gates, a_log, dt_bias, out_norm):
    bsz, seq, _ = q.shape
    ts = GDN_TS
    nh = A_HEADS
    nc = ts // A_CHUNK
    act = pl.BlockSpec((1, ts, A_QK), lambda b, s: (b, s, 0))
    col = pl.BlockSpec((nh, 1), lambda b, s: (0, 0))
    return pl.pallas_call(
        _gdn_kernel,
        grid=(bsz, seq // ts),
        in_specs=[act, act, act, act, pl.BlockSpec((1, ts, LANES), lambda b, s: (b, s, 0)),
                  col, col, pl.BlockSpec((1, LANES), lambda b, s: (0, 0))],
        out_specs=act,
        out_shape=jax.ShapeDtypeStruct((bsz, seq, A_QK), BF16),
        scratch_shapes=[pltpu.VMEM((nh, A_DIM, A_DIM), F32),
                        pltpu.VMEM((nh, ts), F32),
                        pltpu.VMEM((ts, LANES), F32),
                        pltpu.VMEM((nh, nc, 2 * A_CHUNK, A_DIM), BF16),
                        pltpu.VMEM((nh, ts, A_DIM), F32),
                        pltpu.VMEM((nh, nc, A_CHUNK, 2 * A_CHUNK), BF16),
                        pltpu.VMEM((nh, nc, A_DIM, 2 * A_CHUNK), BF16),
                        pltpu.VMEM((nh, ts, A_DIM), BF16)],
        compiler_params=pltpu.CompilerParams(
            dimension_semantics=("parallel", "arbitrary"), vmem_limit_bytes=VMEM_LIMIT),
        name="gated_delta_rule",
    )(q, k, v, z, gates, a_log.reshape(nh, 1), dt_bias.reshape(nh, 1),
      out_norm.reshape(1, LANES))


def _attend(q_fn, kv, bias, nrows):
    dh = B_HEAD_DIM
    lane = lax.broadcasted_iota(jnp.int32, (nrows, LANES), 1)
    m_tile = jnp.zeros((nrows, LANES), F32)
    d_tile = jnp.ones((nrows, LANES), F32)
    pvs = []
    for g in range(B_KV_HEADS):
        qs = jnp.concatenate([q_fn(g * B_REP + r) for r in range(B_REP)], axis=0)
        s = _dot_nt(qs, kv[:, g * dh:(g + 1) * dh]) + bias
        m = jnp.max(s, axis=-1, keepdims=True)
        e = jnp.exp2((s - m) * ATTN_C2)
        den = jnp.sum(e, axis=-1, keepdims=True)
        pv = _dot(e.astype(BF16), kv[:, B_KW + g * dh:B_KW + (g + 1) * dh])
        for r in range(B_REP):
            rows = slice(r * nrows, (r + 1) * nrows)
            m_tile = jnp.where(lane == g * B_REP + r, m[rows], m_tile)
            d_tile = jnp.where(lane == g * B_REP + r, den[rows], d_tile)
            pvs.append(pv[rows])
    return m_tile, d_tile, pvs


def _attn_kernel(q0_ref, q1_ref, q2_ref, kv0_ref, kv1_ref, kv2_ref, bias_ref, bias2_ref,
                 o_ref, pv_ref, m_ref, l_ref):
    t = pl.program_id(1)
    blk = B_BLK
    dh = B_HEAD_DIM
    tq = o_ref.shape[0]

    def head(ref, r, rows):
        return lambda hd: ref[0, r, rows, hd * dh:(hd + 1) * dh]

    def window(ref, r, first_row):
        cur = pl.multiple_of(first_row, blk)
        prev = pl.multiple_of(jnp.maximum(first_row - blk, 0), blk)
        return jnp.concatenate([ref[0, r, pl.ds(prev, blk), :], ref[0, r, pl.ds(cur, blk), :]],
                               axis=0)

    def keep(g, idx, part):
        m_tile, d_tile, pvs = part
        m_ref[g, idx, :] = m_tile
        l_ref[g, idx, :] = d_tile
        for hd, pv in enumerate(pvs):
            pv_ref[g, hd, idx, :] = pv

    for i in range(tq // blk):
        rows = slice(i * blk, (i + 1) * blk)
        bias = bias_ref[jnp.minimum(t, 1)] if i == 0 else bias_ref[1]
        keep(0, rows, _attend(head(q0_ref, 0, rows), window(kv0_ref, 0, t * tq + i * blk),
                              bias, blk))
    d1 = q1_ref.shape[1]
    n1 = tq // d1
    for r in range(d1):
        keep(1, pl.ds(r, n1, stride=d1),
             _attend(head(q1_ref, r, slice(None)), window(kv1_ref, r, t * n1),
                     bias_ref[jnp.minimum(t, 1)], n1))
    d2 = q2_ref.shape[1]
    n2 = tq // d2
    bias2 = bias2_ref[t]
    for r in range(d2):
        keep(2, pl.ds(r, n2, stride=d2),
             _attend(head(q2_ref, r, slice(None)), kv2_ref[0, r], bias2, n2))

    ms = [m_ref[g] for g in range(B_N_GROUPS)]
    m_all = functools.reduce(jnp.maximum, ms)
    ws = [jnp.exp2((m - m_all) * ATTN_C2) for m in ms]
    inv_l = 1.0 / functools.reduce(jnp.add, [w * l_ref[g] for g, w in enumerate(ws)])
    ws = [w * inv_l for w in ws]
    for hd in range(B_HEADS):
        o = ws[0][:, hd:hd + 1] * pv_ref[0, hd]
        for g in range(1, B_N_GROUPS):
            o = o + ws[g][:, hd:hd + 1] * pv_ref[g, hd]
        o_ref[:, hd * dh:(hd + 1) * dh] = o.astype(o_ref.dtype)


def _band_bias(seq):
    qi = jnp.arange(B_REP * B_BLK)[:, None] & (B_BLK - 1)
    kj = jnp.arange(2 * B_BLK)[None, :]
    dist = qi + B_BLK - kj
    band = (dist >= 0) & (dist <= B_BLK)
    first = band & (kj >= B_BLK)
    bias = jnp.where(jnp.stack([first, band]), 0.0, -jnp.inf).astype(F32)
    dil = B_GROUPS[-1][1]
    n2 = ATTN_TQ // dil
    t = jnp.arange(seq // ATTN_TQ)[:, None, None]
    pos = t * n2 + (jnp.arange(B_REP * n2)[None, :, None] % n2)
    dist2 = pos - kj[None]
    bias2 = jnp.where((dist2 >= 0) & (dist2 <= B_BLK), 0.0, -jnp.inf).astype(F32)
    return bias, bias2


def _dilated_attention(qs, kvs, biases, seq):
    bsz = qs[0].shape[0]
    tq = ATTN_TQ
    assert seq // B_GROUPS[-1][1] == 2 * B_BLK
    nt = seq // tq

    def q_spec(a):
        dil = a.shape[1]
        return pl.BlockSpec((1, dil, tq // dil, B_OUT), lambda b, t: (b, 0, t, 0))

    def kv_spec(a):
        return pl.BlockSpec((1,) + a.shape[1:], lambda b, t: (b, 0, 0, 0))

    def whole(a):
        return pl.BlockSpec(a.shape, lambda b, t: (0, 0, 0))

    return pl.pallas_call(
        _attn_kernel,
        grid=(bsz, nt),
        in_specs=[q_spec(a) for a in qs] + [kv_spec(a) for a in kvs] + [whole(a) for a in biases],
        out_specs=pl.BlockSpec((tq, B_OUT), lambda b, t: (b * nt + t, 0)),
        out_shape=jax.ShapeDtypeStruct((bsz * seq, B_OUT), BF16),
        scratch_shapes=[pltpu.VMEM((B_N_GROUPS, B_HEADS, tq, B_HEAD_DIM), F32),
                        pltpu.VMEM((B_N_GROUPS, tq, LANES), F32),
                        pltpu.VMEM((B_N_GROUPS, tq, LANES), F32)],
        compiler_params=pltpu.CompilerParams(
            dimension_semantics=("parallel", "arbitrary"), vmem_limit_bytes=VMEM_LIMIT),
        name="dilated_attention",
    )(*qs, *kvs, *biases)


def _tail_kernel(x_ref, mix_ref, p_ref, wo_ref, gm_ref, wup_ref, wdn_ref, wg_ref, wp_ref, gf_ref,
                 out_ref, *, final):
    x1 = x_ref[...] + _dot(mix_ref[...], wo_ref[...])
    h = _rms(x1, gm_ref[...]).astype(BF16)
    acc = x1
    for f in range(D_FF // TAIL_TF):
        fs = slice(f * TAIL_TF, (f + 1) * TAIL_TF)
        u = jnp.maximum(_dot(h, wup_ref[:, fs]), 0.0)
        acc = acc + _dot((u * u).astype(BF16), wdn_ref[fs, :])
    gate = _sigmoid(_dot(acc.astype(BF16), wg_ref[...]))
    x3 = acc + gate * _dot(p_ref[...].astype(BF16), wp_ref[...])
    if final:
        x3 = _rms(x3, gf_ref[...])
    out_ref[...] = x3


def _layer_tail(x, mix, p, w_o, g_mlp, w_up, w_dn, w_gate, w_ple, g_final, *, final):
    m, d = x.shape
    tm = TAIL_TM

    def rows(width):
        return pl.BlockSpec((tm, width), lambda i: (i, 0))

    def resident(a):
        return pl.BlockSpec(a.shape, lambda i: (0, 0), pipeline_mode=pl.Buffered(1))

    weights = (w_o, g_mlp.reshape(1, d), w_up, w_dn, w_gate, w_ple, g_final.reshape(1, d))
    return pl.pallas_call(
        functools.partial(_tail_kernel, final=final),
        grid=(m // tm,),
        in_specs=[rows(d), rows(d), rows(PLE_DIM)] + [resident(w) for w in weights],
        out_specs=rows(d),
        out_shape=jax.ShapeDtypeStruct((m, d), F32),
        compiler_params=pltpu.CompilerParams(
            dimension_semantics=("parallel",), vmem_limit_bytes=VMEM_LIMIT),
        name="layer_tail",
    )(x, mix, p, *weights)


def _rope_tables(seq):
    inv = jnp.power(jnp.float32(ROPE_THETA), -jnp.arange(0, ROPE_DIM, 2, dtype=F32) / ROPE_DIM)
    ang = jnp.arange(seq, dtype=F32)[:, None] * inv[None, :]
    cos, sin = jnp.cos(ang), jnp.sin(ang)
    zeros = jnp.zeros((seq, LANES - ROPE_DIM), F32)
    zhalf = jnp.zeros((seq, ROPE_HALF), F32)
    c = jnp.concatenate([cos, cos, jnp.ones_like(zeros)], axis=1)
    s_lo = jnp.concatenate([-sin, zhalf, zeros], axis=1)
    s_hi = jnp.concatenate([zhalf, sin, zeros], axis=1)
    return c, s_lo, s_hi


def _group_kv_columns(w_kv):
    d = w_kv.shape[0]
    w = w_kv.reshape(d, 2, B_N_GROUPS, B_KW)
    return jnp.transpose(w, (0, 2, 1, 3)).reshape(d, 2 * B_N_GROUPS * B_KW)


def kernel(x, p, attn_norm, mlp_norm, a_w_in, a_conv_w, a_log, a_dt_bias, a_out_norm, a_w_out,
           kv_norm, b_w_kv, b_w_q, b_w_o, mlp_w_up, mlp_w_down, ple_w_proj, ple_w_gate, final_norm):
    bsz, seq, d = x.shape
    m = bsz * seq
    xf = x.reshape(m, d)
    rope = _rope_tables(seq)
    biases = _band_bias(seq)
    kvs = None
    for i in range(DEPTH):
        if i < N_A_LAYERS:
            w_in = jnp.pad(a_w_in[i], ((0, 0), (0, A_IN_PAD - A_IN_COLS))).astype(BF16)
            q, k, v, z, gates = _a_projection(xf, seq, attn_norm[i], w_in, a_conv_w[i])
            mix = _gated_deltanet(*(t.reshape(bsz, seq, -1) for t in (q, k, v, z, gates)),
                                  a_log[i], a_dt_bias[i], a_out_norm[i]).reshape(m, A_QK)
            w_o = a_w_out[i]
        else:
            j = i - N_A_LAYERS
            w_q = b_w_q[j].astype(BF16)
            if kvs is None:
                outs = _b_projection(xf, bsz, seq, rope, attn_norm[i], w_q, kv_norm,
                                     _group_kv_columns(b_w_kv).astype(BF16))
                qs, kvs = outs[:B_N_GROUPS], outs[B_N_GROUPS:]
            else:
                qs = _b_projection(xf, bsz, seq, rope, attn_norm[i], w_q)
            mix = _dilated_attention(qs, kvs, biases, seq)
            w_o = b_w_o[j]
        xf = _layer_tail(xf, mix, p[i].reshape(m, PLE_DIM), w_o.astype(BF16),
                         mlp_norm[i], mlp_w_up[i].astype(BF16), mlp_w_down[i].astype(BF16),
                         ple_w_gate[i].astype(BF16), ple_w_proj[i].astype(BF16), final_norm,
                         final=(i == DEPTH - 1))
    return xf.reshape(bsz, seq, d)
```

```python
import functools

import jax
import jax.numpy as jnp
import numpy as np
from jax import lax
from jax.experimental import pallas as pl
from jax.experimental.pallas import tpu as pltpu

F32 = jnp.float32
BF16 = jnp.bfloat16

D_MODEL = 1024
DEPTH = 4
N_A_LAYERS = DEPTH // 2
PLE_DIM = 256
D_FF = 4 * D_MODEL
NORM_EPS = 1e-6

A_HEADS = 8
A_DIM = 128
A_CONV = 4
A_CHUNK = 64
A_QK = A_HEADS * A_DIM
A_IN_COLS = 4 * A_QK + 2 * A_HEADS
LANES = 128
A_IN_PAD = 4 * A_QK + LANES

B_HEAD_DIM = 128
B_GROUPS = ((128, 1), (512, 4), (2048, 16))
B_N_GROUPS = len(B_GROUPS)
B_HEADS = 8
B_KV_HEADS = 2
B_REP = B_HEADS // B_KV_HEADS
B_BLK = 128
B_OUT = B_HEADS * B_HEAD_DIM
B_KW = B_KV_HEADS * B_HEAD_DIM
ROPE_THETA = 500000.0
ROPE_DIM = B_HEAD_DIM // 4
ROPE_HALF = ROPE_DIM // 2

VMEM_LIMIT = 56 * 1024 * 1024

PROJ_TM = 512
A_PROJ_TN = 256
GDN_TS = 256
ATTN_TQ = 512
ATTN_C2 = (B_HEAD_DIM ** -0.5) * 1.4426950408889634
TAIL_TM = 512
TAIL_TF = 512


def _dot(a, b):
    return jnp.dot(a, b, preferred_element_type=F32)


def _dot_nt(a, b):
    return lax.dot_general(a, b, (((1,), (1,)), ((), ())), preferred_element_type=F32)


def _rms(x, gain):
    return x * lax.rsqrt(jnp.mean(x * x, axis=-1, keepdims=True) + NORM_EPS) * gain


def _sigmoid(x):
    return 1.0 / (1.0 + jnp.exp(-x))


def _conv_silu(y, hist_ref, prev8, w):
    tm = y.shape[0]
    hist_ref[0:8, :] = prev8
    hist_ref[8:8 + tm, :] = y
    acc = y * w[A_CONV - 1:A_CONV, :]
    for s in range(1, A_CONV):
        acc = acc + hist_ref[8 - s:8 - s + tm, :] * w[A_CONV - 1 - s:A_CONV - s, :]
    return acc * _sigmoid(acc)


def _l2(x):
    return x * lax.rsqrt(jnp.sum(x * x, axis=-1, keepdims=True) + NORM_EPS)


def _aproj_kernel(x_ref, g_ref, w_ref, cw_ref, q_ref, k_ref, v_ref, z_ref, gate_ref,
                  carry_ref, hist_ref, *, sblk):
    tm = x_ref.shape[0]
    wd = A_PROJ_TN

    @pl.when(pl.program_id(0) % sblk == 0)
    def _():
        carry_ref[...] = jnp.zeros_like(carry_ref)

    h = _rms(x_ref[...], g_ref[...]).astype(BF16)
    for t, out_ref in enumerate((q_ref, k_ref, v_ref)):
        for j in range(A_QK // wd):
            cs = slice(j * wd, (j + 1) * wd)
            ws = slice(t * A_QK + j * wd, t * A_QK + (j + 1) * wd)
            y = _dot(h, w_ref[:, ws])
            act = _conv_silu(y, hist_ref, carry_ref[t, :, cs], cw_ref[:, ws])
            carry_ref[t, :, cs] = y[tm - 8:tm, :]
            for hd in range(wd // A_DIM):
                hs = slice(hd * A_DIM, (hd + 1) * A_DIM)
                piece = act[:, hs]
                if t == 0:
                    piece = _l2(piece) * (A_DIM ** -0.5)
                elif t == 1:
                    piece = _l2(piece)
                out_ref[:, j * wd + hd * A_DIM:j * wd + (hd + 1) * A_DIM] = piece
    for j in range(A_QK // wd):
        cs = slice(j * wd, (j + 1) * wd)
        z = _dot(h, w_ref[:, 3 * A_QK + j * wd:3 * A_QK + (j + 1) * wd])
        z_ref[:, cs] = z * _sigmoid(z)
    gate_ref[...] = _dot(h, w_ref[:, 4 * A_QK:])


def _a_projection(x, seq, gain, w_in, conv_w):
    m, d = x.shape
    tm = PROJ_TM
    rows = pl.BlockSpec((tm, A_QK), lambda i: (i, 0))
    act = jax.ShapeDtypeStruct((m, A_QK), F32)
    return pl.pallas_call(
        functools.partial(_aproj_kernel, sblk=seq // tm),
        grid=(m // tm,),
        in_specs=[pl.BlockSpec((tm, d), lambda i: (i, 0)),
                  pl.BlockSpec((1, d), lambda i: (0, 0)),
                  pl.BlockSpec(w_in.shape, lambda i: (0, 0)),
                  pl.BlockSpec(conv_w.shape, lambda i: (0, 0))],
        out_specs=[rows, rows, rows, rows, pl.BlockSpec((tm, LANES), lambda i: (i, 0))],
        out_shape=[act, act, act, act, jax.ShapeDtypeStruct((m, LANES), F32)],
        scratch_shapes=[pltpu.VMEM((3, 8, A_QK), F32),
                        pltpu.VMEM((tm + 8, A_PROJ_TN), F32)],
        compiler_params=pltpu.CompilerParams(
            dimension_semantics=("arbitrary",), vmem_limit_bytes=VMEM_LIMIT),
        name="gdn_projection",
    )(x, gain.reshape(1, d), w_in, conv_w)


def _rope(y, c, s):
    return y * c + pltpu.roll(y, LANES // 2, axis=1) * s


def _store_residue_major(y, n_rope, tabs, dilation, out_ref, scr_ref):
    tm, w = y.shape
    for b in range(w // LANES):
        sl = slice(b * LANES, (b + 1) * LANES)
        blk = _rope(y[:, sl], *tabs) if b < n_rope else y[:, sl]
        if dilation == 1:
            out_ref[0, 0, :, sl] = blk.astype(out_ref.dtype)
        else:
            scr_ref[b] = blk
    if dilation > 1:
        rows = tm // dilation
        for r in range(dilation):
            for b in range(w // LANES):
                out_ref[0, r, :, b * LANES:(b + 1) * LANES] = scr_ref[
                    b, pl.ds(r, rows, stride=dilation), :].astype(out_ref.dtype)


def _bproj_kernel(*refs, with_kv):
    if with_kv:
        (x_ref, gq_ref, wq_ref, gkv_ref, wkv_ref, c_ref, s_ref,
         q0_ref, q1_ref, q2_ref, kv0_ref, kv1_ref, kv2_ref, scr_ref) = refs
    else:
        (x_ref, gq_ref, wq_ref, c_ref, s_ref,
         q0_ref, q1_ref, q2_ref, scr_ref) = refs
    x = x_ref[...]
    xn = x * lax.rsqrt(jnp.mean(x * x, axis=-1, keepdims=True) + NORM_EPS)
    tabs = (c_ref[...], s_ref[...])
    hq = (xn * gq_ref[...]).astype(BF16)
    for g, (q_ref, (_, dil)) in enumerate(zip((q0_ref, q1_ref, q2_ref), B_GROUPS)):
        y = _dot(hq, wq_ref[:, g * B_OUT:(g + 1) * B_OUT])
        _store_residue_major(y, B_HEADS, tabs, dil, q_ref, scr_ref)
    if with_kv:
        hkv = (xn * gkv_ref[...]).astype(BF16)
        for g, (kv_ref, (_, dil)) in enumerate(zip((kv0_ref, kv1_ref, kv2_ref), B_GROUPS)):
            y = _dot(hkv, wkv_ref[:, g * 2 * B_KW:(g + 1) * 2 * B_KW])
            _store_residue_major(y, B_KV_HEADS, tabs, dil, kv_ref, scr_ref)


def _b_projection(x, bsz, seq, rope, g_q, w_q, g_kv=None, w_kv=None):
    m, d = x.shape
    tm = PROJ_TM
    sblk = seq // tm
    with_kv = w_kv is not None

    def whole(a):
        return pl.BlockSpec(a.shape, lambda i: (0, 0))

    def out(width, dil):
        return pl.BlockSpec((1, dil, tm // dil, width), lambda i: (i // sblk, 0, i % sblk, 0))

    def out_shape(width, dil):
        return jax.ShapeDtypeStruct((bsz, dil, seq // dil, width), BF16)

    tab = pl.BlockSpec((tm, LANES), lambda i: (i % sblk, 0))
    args = [x, g_q.reshape(1, d), w_q]
    in_specs = [pl.BlockSpec((tm, d), lambda i: (i, 0)), whole(args[1]), whole(w_q)]
    if with_kv:
        args += [g_kv.reshape(1, d), w_kv]
        in_specs += [whole(args[3]), whole(w_kv)]
    args += list(rope)
    in_specs += [tab] * len(rope)
    widths = [B_OUT] * B_N_GROUPS + ([2 * B_KW] * B_N_GROUPS if with_kv else [])
    dils = [dil for _, dil in B_GROUPS] * (2 if with_kv else 1)
    return pl.pallas_call(
        functools.partial(_bproj_kernel, with_kv=with_kv),
        grid=(m // tm,),
        in_specs=in_specs,
        out_specs=[out(w, dl) for w, dl in zip(widths, dils)],
        out_shape=[out_shape(w, dl) for w, dl in zip(widths, dils)],
        scratch_shapes=[pltpu.VMEM((B_HEADS, tm, LANES), F32)],
        compiler_params=pltpu.CompilerParams(
            dimension_semantics=("parallel",), vmem_limit_bytes=VMEM_LIMIT),
        name="attn_projection",
    )(*args)


_G_BETA, _G_CUM, _G_EXP, _G_KDEC, _G_LAST = (i * A_HEADS for i in range(5))


def _gdn_kernel(q_ref, k_ref, v_ref, z_ref, gate_ref, alog_ref, dt_ref, onorm_ref,
                o_ref, state_ref, gt_ref, gs_ref, lhs_ref, u_ref, attn_ref, kdt_ref, vnew_ref):
    ts = q_ref.shape[1]
    c = A_CHUNK
    nc = ts // c
    nh = A_HEADS

    @pl.when(pl.program_id(1) == 0)
    def _():
        state_ref[...] = jnp.zeros_like(state_ref)

    gates_t = gate_ref[0].T
    a = gates_t[nh:2 * nh] + dt_ref[...]
    softplus = jnp.maximum(a, 0.0) + jnp.log(1.0 + jnp.exp(-jnp.abs(a)))
    g = -jnp.exp(alog_ref[...]) * softplus
    lane_in_chunk = lax.broadcasted_iota(jnp.int32, (nh, ts), 1) & (c - 1)
    s = 1
    while s < c:
        g = g + jnp.where(lane_in_chunk >= s, pltpu.roll(g, s, axis=1), 0.0)
        s *= 2
    g_last = jnp.concatenate(
        [jnp.broadcast_to(g[:, (n + 1) * c - 1:(n + 1) * c], (nh, c)) for n in range(nc)], axis=1)
    rows = jnp.concatenate(
        [_sigmoid(gates_t[0:nh]), g, jnp.exp(g), jnp.exp(g_last - g), jnp.exp(g_last),
         jnp.zeros((LANES - 5 * nh, ts), F32)], axis=0)
    gt_ref[...] = g
    gs_ref[...] = rows.T
    vnew_ref[...] = jnp.zeros_like(vnew_ref)

    ri = lax.broadcasted_iota(jnp.int32, (c, ts), 0)
    ci = lax.broadcasted_iota(jnp.int32, (c, ts), 1)
    incl = ri >= (ci & (c - 1))
    strict = ri > (ci & (c - 1))
    eye = jnp.where(ri == (ci & (c - 1)), 1.0, 0.0)
    chunk_of_lane = ci // c
    cat_mask = [jnp.where(chunk_of_lane == n, 1.0, 0.0).astype(BF16) for n in range(nc)]
    wide_lane = lax.broadcasted_iota(jnp.int32, (A_DIM, ts), 1) // c
    wide_mask = [jnp.where(wide_lane == n, 1.0, 0.0).astype(BF16) for n in range(nc)]

    def block_diag(cat):
        return jnp.concatenate([cat * cat_mask[n] for n in range(nc)], axis=0)

    inv, power = {}, {}
    r_all, qdec = {}, {}
    for h in range(nh):
        hs = slice(h * A_DIM, (h + 1) * A_DIM)
        q, k, v = q_ref[0, :, hs], k_ref[0, :, hs], v_ref[0, :, hs]
        bc = gs_ref[:, _G_BETA + h:_G_BETA + h + 1]
        gc = gs_ref[:, _G_CUM + h:_G_CUM + h + 1]
        egc = gs_ref[:, _G_EXP + h:_G_EXP + h + 1]
        kdf = gs_ref[:, _G_KDEC + h:_G_KDEC + h + 1]
        g_row = gt_ref[h:h + 1, :]
        kb = k * bc
        kt = k.T.astype(BF16)
        k_rhs = jnp.concatenate([kt * wide_mask[n] for n in range(nc)], axis=0)
        lhs = jnp.concatenate(
            [jnp.concatenate([kb[n * c:(n + 1) * c], q[n * c:(n + 1) * c]], axis=0)
             for n in range(nc)], axis=1).astype(BF16)
        qk = _dot(lhs, k_rhs)
        gc_cat = jnp.broadcast_to(gc[(nc - 1) * c:nc * c], (c, ts))
        for n in range(nc - 2, -1, -1):
            gc_cat = jnp.where(chunk_of_lane == n,
                               jnp.broadcast_to(gc[n * c:(n + 1) * c], (c, ts)), gc_cat)
        decay = jnp.exp(jnp.where(incl, gc_cat - g_row, -jnp.inf))
        lower = jnp.where(strict, qk[:c] * decay, 0.0)
        attn = jnp.where(incl, qk[c:] * decay, 0.0).astype(BF16)
        kdt = (k * kdf).T.astype(BF16)
        for n in range(nc):
            pair = slice((n // 2) * 2 * c, (n // 2 + 1) * 2 * c)
            attn_ref[h, n] = (attn * cat_mask[n])[:, pair]
            kdt_ref[h, n] = (kdt * wide_mask[n])[:, pair]
        r_all[h] = jnp.concatenate([v * bc, kb * egc], axis=1).astype(BF16)
        qdec[h] = (q * egc).astype(BF16)
        inv[h] = eye - lower
        power[h] = lower.astype(BF16)
    diag = {h: block_diag(power[h]) for h in range(nh)}
    p = 2
    while p < c:
        for h in range(nh):
            power[h] = _dot(power[h], diag[h]).astype(BF16)
            diag[h] = block_diag(power[h])
            inv[h] = inv[h] + _dot(inv[h].astype(BF16), diag[h])
        p *= 2
    for h in range(nh):
        uw = _dot(block_diag(inv[h].astype(BF16)), r_all[h])
        u_ref[h] = uw[:, :A_DIM]
        w = uw[:, A_DIM:].astype(BF16)
        for n in range(nc):
            sl = slice(n * c, (n + 1) * c)
            lhs_ref[h, n] = jnp.concatenate([w[sl], qdec[h][sl]], axis=0)

    onorm = onorm_ref[...]
    for n in range(nc):
        sl = slice(n * c, (n + 1) * c)
        pair = slice((n // 2) * 2 * c, (n // 2 + 1) * 2 * c)
        ws = [_dot(lhs_ref[h, n], state_ref[h].astype(BF16)) for h in range(nh)]
        for h in range(nh):
            hs = slice(h * A_DIM, (h + 1) * A_DIM)
            vnew_ref[h, sl, :] = (u_ref[h, sl, :] - ws[h][:c]).astype(BF16)
            v_pair = vnew_ref[h, pair, :]
            o = ws[h][c:] + _dot(attn_ref[h, n], v_pair)
            chunk_decay = gs_ref[n * c:n * c + 1, _G_LAST + h:_G_LAST + h + 1]
            state_ref[h] = state_ref[h] * chunk_decay + _dot(kdt_ref[h, n], v_pair)
            o = o * lax.rsqrt(jnp.mean(o * o, axis=-1, keepdims=True) + NORM_EPS)
            o_ref[0, sl, hs] = (o * onorm * z_ref[0, sl, hs]).astype(o_ref.dtype)


def _gated_deltanet(q, k, v, z, gates, a_log, dt_bias, out_norm):
    bsz, seq, _ = q.shape
    ts = GDN_TS
    nh = A_HEADS
    nc = ts // A_CHUNK
    act = pl.BlockSpec((1, ts, A_QK), lambda b, s: (b, s, 0))
    col = pl.BlockSpec((nh, 1), lambda b, s: (0, 0))
    return pl.pallas_call(
        _gdn_kernel,
        grid=(bsz, seq // ts),
        in_specs=[act, act, act, act, pl.BlockSpec((1, ts, LANES), lambda b, s: (b, s, 0)),
                  col, col, pl.BlockSpec((1, LANES), lambda b, s: (0, 0))],
        out_specs=act,
        out_shape=jax.ShapeDtypeStruct((bsz, seq, A_QK), BF16),
        scratch_shapes=[pltpu.VMEM((nh, A_DIM, A_DIM), F32),
                        pltpu.VMEM((nh, ts), F32),
                        pltpu.VMEM((ts, LANES), F32),
                        pltpu.VMEM((nh, nc, 2 * A_CHUNK, A_DIM), BF16),
                        pltpu.VMEM((nh, ts, A_DIM), F32),
                        pltpu.VMEM((nh, nc, A_CHUNK, 2 * A_CHUNK), BF16),
                        pltpu.VMEM((nh, nc, A_DIM, 2 * A_CHUNK), BF16),
                        pltpu.VMEM((nh, ts, A_DIM), BF16)],
        compiler_params=pltpu.CompilerParams(
            dimension_semantics=("parallel", "arbitrary"), vmem_limit_bytes=VMEM_LIMIT),
        name="gated_delta_rule",
    )(q, k, v, z, gates, a_log.reshape(nh, 1), dt_bias.reshape(nh, 1),
      out_norm.reshape(1, LANES))


def _attend(q_fn, kv, bias, nrows):
    dh = B_HEAD_DIM
    lane = lax.broadcasted_iota(jnp.int32, (nrows, LANES), 1)
    m_tile = jnp.zeros((nrows, LANES), F32)
    d_tile = jnp.ones((nrows, LANES), F32)
    pvs = []
    for g in range(B_KV_HEADS):
        qs = jnp.concatenate([q_fn(g * B_REP + r) for r in range(B_REP)], axis=0)
        s = _dot_nt(qs, kv[:, g * dh:(g + 1) * dh]) + bias
        m = jnp.max(s, axis=-1, keepdims=True)
        e = jnp.exp2((s - m) * ATTN_C2)
        den = jnp.sum(e, axis=-1, keepdims=True)
        pv = _dot(e.astype(BF16), kv[:, B_KW + g * dh:B_KW + (g + 1) * dh])
        for r in range(B_REP):
            rows = slice(r * nrows, (r + 1) * nrows)
            m_tile = jnp.where(lane == g * B_REP + r, m[rows], m_tile)
            d_tile = jnp.where(lane == g * B_REP + r, den[rows], d_tile)
            pvs.append(pv[rows])
    return m_tile, d_tile, pvs


def _attn_kernel(q0_ref, q1_ref, q2_ref, kv0_ref, kv1_ref, kv2_ref, bias_ref, bias2_ref,
                 o_ref, pv_ref, m_ref, l_ref):
    t = pl.program_id(1)
    blk = B_BLK
    dh = B_HEAD_DIM
    tq = o_ref.shape[0]

    def head(ref, r, rows):
        return lambda hd: ref[0, r, rows, hd * dh:(hd + 1) * dh]

    def window(ref, r, first_row):
        cur = pl.multiple_of(first_row, blk)
        prev = pl.multiple_of(jnp.maximum(first_row - blk, 0), blk)
        return jnp.concatenate([ref[0, r, pl.ds(prev, blk), :], ref[0, r, pl.ds(cur, blk), :]],
                               axis=0)

    def keep(g, idx, part):
        m_tile, d_tile, pvs = part
        m_ref[g, idx, :] = m_tile
        l_ref[g, idx, :] = d_tile
        for hd, pv in enumerate(pvs):
            pv_ref[g, hd, idx, :] = pv

    for i in range(tq // blk):
        rows = slice(i * blk, (i + 1) * blk)
        bias = bias_ref[jnp.minimum(t, 1)] if i == 0 else bias_ref[1]
        keep(0, rows, _attend(head(q0_ref, 0, rows), window(kv0_ref, 0, t * tq + i * blk),
                              bias, blk))
    d1 = q1_ref.shape[1]
    n1 = tq // d1
    for r in range(d1):
        keep(1, pl.ds(r, n1, stride=d1),
             _attend(head(q1_ref, r, slice(None)), window(kv1_ref, r, t * n1),
                     bias_ref[jnp.minimum(t, 1)], n1))
    d2 = q2_ref.shape[1]
    n2 = tq // d2
    bias2 = bias2_ref[t]
    for r in range(d2):
        keep(2, pl.ds(r, n2, stride=d2),
             _attend(head(q2_ref, r, slice(None)), kv2_ref[0, r], bias2, n2))

    ms = [m_ref[g] for g in range(B_N_GROUPS)]
    m_all = functools.reduce(jnp.maximum, ms)
    ws = [jnp.exp2((m - m_all) * ATTN_C2) for m in ms]
    inv_l = 1.0 / functools.reduce(jnp.add, [w * l_ref[g] for g, w in enumerate(ws)])
    ws = [w * inv_l for w in ws]
    for hd in range(B_HEADS):
        o = ws[0][:, hd:hd + 1] * pv_ref[0, hd]
        for g in range(1, B_N_GROUPS):
            o = o + ws[g][:, hd:hd + 1] * pv_ref[g, hd]
        o_ref[:, hd * dh:(hd + 1) * dh] = o.astype(o_ref.dtype)


def _band_bias(seq):
    qi = np.arange(B_REP * B_BLK)[:, None] & (B_BLK - 1)
    kj = np.arange(2 * B_BLK)[None, :]
    dist = qi + B_BLK - kj
    band = (dist >= 0) & (dist <= B_BLK)
    first = band & (kj >= B_BLK)
    bias = np.where(np.stack([first, band]), 0.0, -np.inf)
    dil = B_GROUPS[-1][1]
    n2 = ATTN_TQ // dil
    t = np.arange(seq // ATTN_TQ)[:, None, None]
    pos = t * n2 + (np.arange(B_REP * n2)[None, :, None] % n2)
    dist2 = pos - kj[None]
    bias2 = np.where((dist2 >= 0) & (dist2 <= B_BLK), 0.0, -np.inf)
    return jnp.asarray(bias, F32), jnp.asarray(bias2, F32)


def _dilated_attention(qs, kvs, biases, seq):
    bsz = qs[0].shape[0]
    tq = ATTN_TQ
    assert seq // B_GROUPS[-1][1] == 2 * B_BLK
    nt = seq // tq

    def q_spec(a):
        dil = a.shape[1]
        return pl.BlockSpec((1, dil, tq // dil, B_OUT), lambda b, t: (b, 0, t, 0))

    def kv_spec(a):
        return pl.BlockSpec((1,) + a.shape[1:], lambda b, t: (b, 0, 0, 0))

    def whole(a):
        return pl.BlockSpec(a.shape, lambda b, t: (0, 0, 0))

    return pl.pallas_call(
        _attn_kernel,
        grid=(bsz, nt),
        in_specs=[q_spec(a) for a in qs] + [kv_spec(a) for a in kvs] + [whole(a) for a in biases],
        out_specs=pl.BlockSpec((tq, B_OUT), lambda b, t: (b * nt + t, 0)),
        out_shape=jax.ShapeDtypeStruct((bsz * seq, B_OUT), BF16),
        scratch_shapes=[pltpu.VMEM((B_N_GROUPS, B_HEADS, tq, B_HEAD_DIM), F32),
                        pltpu.VMEM((B_N_GROUPS, tq, LANES), F32),
                        pltpu.VMEM((B_N_GROUPS, tq, LANES), F32)],
        compiler_params=pltpu.CompilerParams(
            dimension_semantics=("parallel", "arbitrary"), vmem_limit_bytes=VMEM_LIMIT),
        name="dilated_attention",
    )(*qs, *kvs, *biases)


def _tail_kernel(x_ref, mix_ref, p_ref, wo_ref, gm_ref, wup_ref, wdn_ref, wg_ref, wp_ref, gf_ref,
                 out_ref, *, final):
    x1 = x_ref[...] + _dot(mix_ref[...], wo_ref[...])
    h = _rms(x1, gm_ref[...]).astype(BF16)
    acc = x1
    for f in range(D_FF // TAIL_TF):
        fs = slice(f * TAIL_TF, (f + 1) * TAIL_TF)
        u = jnp.maximum(_dot(h, wup_ref[:, fs]), 0.0)
        acc = acc + _dot((u * u).astype(BF16), wdn_ref[fs, :])
    gate = _sigmoid(_dot(acc.astype(BF16), wg_ref[...]))
    x3 = acc + gate * _dot(p_ref[...].astype(BF16), wp_ref[...])
    if final:
        x3 = _rms(x3, gf_ref[...])
    out_ref[...] = x3


def _layer_tail(x, mix, p, w_o, g_mlp, w_up, w_dn, w_gate, w_ple, g_final, *, final):
    m, d = x.shape
    tm = TAIL_TM

    def rows(width):
        return pl.BlockSpec((tm, width), lambda i: (i, 0))

    def resident(a):
        return pl.BlockSpec(a.shape, lambda i: (0, 0), pipeline_mode=pl.Buffered(1))

    weights = (w_o, g_mlp.reshape(1, d), w_up, w_dn, w_gate, w_ple, g_final.reshape(1, d))
    return pl.pallas_call(
        functools.partial(_tail_kernel, final=final),
        grid=(m // tm,),
        in_specs=[rows(d), rows(d), rows(PLE_DIM)] + [resident(w) for w in weights],
        out_specs=rows(d),
        out_shape=jax.ShapeDtypeStruct((m, d), F32),
        compiler_params=pltpu.CompilerParams(
            dimension_semantics=("parallel",), vmem_limit_bytes=VMEM_LIMIT),
        name="layer_tail",
    )(x, mix, p, *weights)


def _rope_tables(seq):
    inv = np.power(np.float32(ROPE_THETA), -np.arange(0, ROPE_DIM, 2, dtype=np.float32) / ROPE_DIM)
    ang = np.arange(seq, dtype=np.float32)[:, None] * inv[None, :].astype(np.float32)
    cos, sin = np.cos(ang), np.sin(ang)
    c = np.ones((seq, LANES), np.float32)
    s = np.zeros((seq, LANES), np.float32)
    c[:, :ROPE_HALF] = cos
    c[:, LANES // 2:LANES // 2 + ROPE_HALF] = cos
    s[:, :ROPE_HALF] = -sin
    s[:, LANES // 2:LANES // 2 + ROPE_HALF] = sin
    return jnp.asarray(c), jnp.asarray(s)


def _rope_head_order():
    d = np.arange(B_HEAD_DIM)
    return np.concatenate([d[:ROPE_HALF], d[ROPE_DIM:LANES // 2 + ROPE_HALF],
                           d[ROPE_HALF:ROPE_DIM], d[LANES // 2 + ROPE_HALF:]])


def _rope_columns(w, n_heads):
    order = (np.arange(n_heads)[:, None] * B_HEAD_DIM + _rope_head_order()[None, :]).reshape(-1)
    return w[:, order]


def _group_kv_columns(w_kv):
    d = w_kv.shape[0]
    w_k = _rope_columns(w_kv[:, :B_N_GROUPS * B_KW], B_N_GROUPS * B_KV_HEADS)
    w = jnp.stack([w_k.reshape(d, B_N_GROUPS, B_KW),
                   w_kv[:, B_N_GROUPS * B_KW:].reshape(d, B_N_GROUPS, B_KW)], axis=2)
    return w.reshape(d, 2 * B_N_GROUPS * B_KW)


def kernel(x, p, attn_norm, mlp_norm, a_w_in, a_conv_w, a_log, a_dt_bias, a_out_norm, a_w_out,
           kv_norm, b_w_kv, b_w_q, b_w_o, mlp_w_up, mlp_w_down, ple_w_proj, ple_w_gate, final_norm):
    bsz, seq, d = x.shape
    m = bsz * seq
    xf = x.reshape(m, d)
    rope = _rope_tables(seq)
    biases = _band_bias(seq)
    kvs = None
    for i in range(DEPTH):
        if i < N_A_LAYERS:
            w_in = jnp.pad(a_w_in[i], ((0, 0), (0, A_IN_PAD - A_IN_COLS))).astype(BF16)
            q, k, v, z, gates = _a_projection(xf, seq, attn_norm[i], w_in, a_conv_w[i])
            mix = _gated_deltanet(*(t.reshape(bsz, seq, -1) for t in (q, k, v, z, gates)),
                                  a_log[i], a_dt_bias[i], a_out_norm[i]).reshape(m, A_QK)
            w_o = a_w_out[i]
        else:
            j = i - N_A_LAYERS
            w_q = _rope_columns(b_w_q[j], B_N_GROUPS * B_HEADS).astype(BF16)
            if kvs is None:
                outs = _b_projection(xf, bsz, seq, rope, attn_norm[i], w_q, kv_norm,
                                     _group_kv_columns(b_w_kv).astype(BF16))
                qs, kvs = outs[:B_N_GROUPS], outs[B_N_GROUPS:]
            else:
                qs = _b_projection(xf, bsz, seq, rope, attn_norm[i], w_q)
            mix = _dilated_attention(qs, kvs, biases, seq)
            w_o = b_w_o[j]
        xf = _layer_tail(xf, mix, p[i].reshape(m, PLE_DIM), w_o.astype(BF16),
                         mlp_norm[i], mlp_w_up[i].astype(BF16), mlp_w_down[i].astype(BF16),
                         ple_w_gate[i].astype(BF16), ple_w_proj[i].astype(BF16), final_norm,
                         final=(i == DEPTH - 1))
    return xf.reshape(bsz, seq, d)
```

```python
import functools

import jax
import jax.numpy as jnp
import numpy as np
from jax import lax
from jax.experimental import pallas as pl
from jax.experimental.pallas import tpu as pltpu

F32 = jnp.float32
BF16 = jnp.bfloat16

D_MODEL = 1024
DEPTH = 4
N_A_LAYERS = DEPTH // 2
PLE_DIM = 256
D_FF = 4 * D_MODEL
NORM_EPS = 1e-6

A_HEADS = 8
A_DIM = 128
A_CONV = 4
A_CHUNK = 64
A_QK = A_HEADS * A_DIM
A_IN_COLS = 4 * A_QK + 2 * A_HEADS
LANES = 128
A_IN_PAD = 4 * A_QK + LANES

B_HEAD_DIM = 128
B_GROUPS = ((128, 1), (512, 4), (2048, 16))
B_N_GROUPS = len(B_GROUPS)
B_HEADS = 8
B_KV_HEADS = 2
B_REP = B_HEADS // B_KV_HEADS
B_BLK = 128
B_OUT = B_HEADS * B_HEAD_DIM
B_KW = B_KV_HEADS * B_HEAD_DIM
ROPE_THETA = 500000.0
ROPE_DIM = B_HEAD_DIM // 4
ROPE_HALF = ROPE_DIM // 2

VMEM_LIMIT = 56 * 1024 * 1024

PROJ_TM = 512
A_PROJ_TN = 256
GDN_TS = 256
ATTN_TQ = 512
ATTN_C2 = (B_HEAD_DIM ** -0.5) * 1.4426950408889634
TAIL_TM = 512
TAIL_TF = 512


def _dot(a, b):
    return jnp.dot(a, b, preferred_element_type=F32)


def _dot_nt(a, b):
    return lax.dot_general(a, b, (((1,), (1,)), ((), ())), preferred_element_type=F32)


def _rms(x, gain):
    return x * lax.rsqrt(jnp.mean(x * x, axis=-1, keepdims=True) + NORM_EPS) * gain


def _sigmoid(x):
    return 1.0 / (1.0 + jnp.exp(-x))


def _conv_silu(y, hist_ref, prev8, w):
    tm = y.shape[0]
    hist_ref[0:8, :] = prev8
    hist_ref[8:8 + tm, :] = y
    acc = y * w[A_CONV - 1:A_CONV, :]
    for s in range(1, A_CONV):
        acc = acc + hist_ref[8 - s:8 - s + tm, :] * w[A_CONV - 1 - s:A_CONV - s, :]
    return acc * _sigmoid(acc)


def _l2(x):
    return x * lax.rsqrt(jnp.sum(x * x, axis=-1, keepdims=True) + NORM_EPS)


def _aproj_kernel(x_ref, g_ref, w_ref, cw_ref, q_ref, k_ref, v_ref, z_ref, gate_ref,
                  carry_ref, hist_ref, *, sblk):
    tm = x_ref.shape[0]
    wd = A_PROJ_TN

    @pl.when(pl.program_id(0) % sblk == 0)
    def _():
        carry_ref[...] = jnp.zeros_like(carry_ref)

    h = _rms(x_ref[...], g_ref[...]).astype(BF16)
    for t, out_ref in enumerate((q_ref, k_ref, v_ref)):
        for j in range(A_QK // wd):
            cs = slice(j * wd, (j + 1) * wd)
            ws = slice(t * A_QK + j * wd, t * A_QK + (j + 1) * wd)
            y = _dot(h, w_ref[:, ws])
            act = _conv_silu(y, hist_ref, carry_ref[t, :, cs], cw_ref[:, ws])
            carry_ref[t, :, cs] = y[tm - 8:tm, :]
            for hd in range(wd // A_DIM):
                hs = slice(hd * A_DIM, (hd + 1) * A_DIM)
                piece = act[:, hs]
                if t == 0:
                    piece = _l2(piece) * (A_DIM ** -0.5)
                elif t == 1:
                    piece = _l2(piece)
                out_ref[:, j * wd + hd * A_DIM:j * wd + (hd + 1) * A_DIM] = piece
    for j in range(A_QK // wd):
        cs = slice(j * wd, (j + 1) * wd)
        z = _dot(h, w_ref[:, 3 * A_QK + j * wd:3 * A_QK + (j + 1) * wd])
        z_ref[:, cs] = z * _sigmoid(z)
    gate_ref[...] = _dot(h, w_ref[:, 4 * A_QK:])


def _a_projection(x, seq, gain, w_in, conv_w):
    m, d = x.shape
    tm = PROJ_TM
    rows = pl.BlockSpec((tm, A_QK), lambda i: (i, 0))
    act = jax.ShapeDtypeStruct((m, A_QK), F32)
    return pl.pallas_call(
        functools.partial(_aproj_kernel, sblk=seq // tm),
        grid=(m // tm,),
        in_specs=[pl.BlockSpec((tm, d), lambda i: (i, 0)),
                  pl.BlockSpec((1, d), lambda i: (0, 0)),
                  pl.BlockSpec(w_in.shape, lambda i: (0, 0)),
                  pl.BlockSpec(conv_w.shape, lambda i: (0, 0))],
        out_specs=[rows, rows, rows, rows, pl.BlockSpec((tm, LANES), lambda i: (i, 0))],
        out_shape=[act, act, act, act, jax.ShapeDtypeStruct((m, LANES), F32)],
        scratch_shapes=[pltpu.VMEM((3, 8, A_QK), F32),
                        pltpu.VMEM((tm + 8, A_PROJ_TN), F32)],
        compiler_params=pltpu.CompilerParams(
            dimension_semantics=("arbitrary",), vmem_limit_bytes=VMEM_LIMIT),
        name="gdn_projection",
    )(x, gain.reshape(1, d), w_in, conv_w)


def _rope(y, c, s_lo, s_hi):
    return (y * c + pltpu.roll(y, LANES - ROPE_HALF, axis=1) * s_lo
            + pltpu.roll(y, ROPE_HALF, axis=1) * s_hi)


def _store_residue_major(y, n_rope, tabs, dilation, out_ref, scr_ref):
    tm, w = y.shape
    for b in range(w // LANES):
        sl = slice(b * LANES, (b + 1) * LANES)
        blk = _rope(y[:, sl], *tabs) if b < n_rope else y[:, sl]
        if dilation == 1:
            out_ref[0, 0, :, sl] = blk.astype(out_ref.dtype)
        else:
            scr_ref[b] = blk
    if dilation > 1:
        rows = tm // dilation
        for r in range(dilation):
            for b in range(w // LANES):
                out_ref[0, r, :, b * LANES:(b + 1) * LANES] = scr_ref[
                    b, pl.ds(r, rows, stride=dilation), :].astype(out_ref.dtype)


def _bproj_kernel(*refs, with_kv):
    if with_kv:
        (x_ref, gq_ref, wq_ref, gkv_ref, wkv_ref, c_ref, slo_ref, shi_ref,
         q0_ref, q1_ref, q2_ref, kv0_ref, kv1_ref, kv2_ref, scr_ref) = refs
    else:
        (x_ref, gq_ref, wq_ref, c_ref, slo_ref, shi_ref,
         q0_ref, q1_ref, q2_ref, scr_ref) = refs
    x = x_ref[...]
    xn = x * lax.rsqrt(jnp.mean(x * x, axis=-1, keepdims=True) + NORM_EPS)
    tabs = (c_ref[...], slo_ref[...], shi_ref[...])
    hq = (xn * gq_ref[...]).astype(BF16)
    for g, (q_ref, (_, dil)) in enumerate(zip((q0_ref, q1_ref, q2_ref), B_GROUPS)):
        y = _dot(hq, wq_ref[:, g * B_OUT:(g + 1) * B_OUT])
        _store_residue_major(y, B_HEADS, tabs, dil, q_ref, scr_ref)
    if with_kv:
        hkv = (xn * gkv_ref[...]).astype(BF16)
        for g, (kv_ref, (_, dil)) in enumerate(zip((kv0_ref, kv1_ref, kv2_ref), B_GROUPS)):
            y = _dot(hkv, wkv_ref[:, g * 2 * B_KW:(g + 1) * 2 * B_KW])
            _store_residue_major(y, B_KV_HEADS, tabs, dil, kv_ref, scr_ref)


def _b_projection(x, bsz, seq, rope, g_q, w_q, g_kv=None, w_kv=None):
    m, d = x.shape
    tm = PROJ_TM
    sblk = seq // tm
    with_kv = w_kv is not None

    def whole(a):
        return pl.BlockSpec(a.shape, lambda i: (0, 0))

    def out(width, dil):
        return pl.BlockSpec((1, dil, tm // dil, width), lambda i: (i // sblk, 0, i % sblk, 0))

    def out_shape(width, dil):
        return jax.ShapeDtypeStruct((bsz, dil, seq // dil, width), BF16)

    tab = pl.BlockSpec((tm, LANES), lambda i: (i % sblk, 0))
    args = [x, g_q.reshape(1, d), w_q]
    in_specs = [pl.BlockSpec((tm, d), lambda i: (i, 0)), whole(args[1]), whole(w_q)]
    if with_kv:
        args += [g_kv.reshape(1, d), w_kv]
        in_specs += [whole(args[3]), whole(w_kv)]
    args += list(rope)
    in_specs += [tab] * len(rope)
    widths = [B_OUT] * B_N_GROUPS + ([2 * B_KW] * B_N_GROUPS if with_kv else [])
    dils = [dil for _, dil in B_GROUPS] * (2 if with_kv else 1)
    return pl.pallas_call(
        functools.partial(_bproj_kernel, with_kv=with_kv),
        grid=(m // tm,),
        in_specs=in_specs,
        out_specs=[out(w, dl) for w, dl in zip(widths, dils)],
        out_shape=[out_shape(w, dl) for w, dl in zip(widths, dils)],
        scratch_shapes=[pltpu.VMEM((B_HEADS, tm, LANES), F32)],
        compiler_params=pltpu.CompilerParams(
            dimension_semantics=("parallel",), vmem_limit_bytes=VMEM_LIMIT),
        name="attn_projection",
    )(*args)


_G_BETA, _G_CUM, _G_EXP, _G_KDEC, _G_LAST = (i * A_HEADS for i in range(5))


def _gdn_kernel(q_ref, k_ref, v_ref, z_ref, gate_ref, alog_ref, dt_ref, onorm_ref,
                o_ref, state_ref, gt_ref, gs_ref, lhs_ref, u_ref, attn_ref, kdt_ref, vnew_ref):
    ts = q_ref.shape[1]
    c = A_CHUNK
    nc = ts // c
    nh = A_HEADS

    @pl.when(pl.program_id(1) == 0)
    def _():
        state_ref[...] = jnp.zeros_like(state_ref)

    gates_t = gate_ref[0].T
    a = gates_t[nh:2 * nh] + dt_ref[...]
    softplus = jnp.maximum(a, 0.0) + jnp.log(1.0 + jnp.exp(-jnp.abs(a)))
    g = -jnp.exp(alog_ref[...]) * softplus
    lane_in_chunk = lax.broadcasted_iota(jnp.int32, (nh, ts), 1) & (c - 1)
    s = 1
    while s < c:
        g = g + jnp.where(lane_in_chunk >= s, pltpu.roll(g, s, axis=1), 0.0)
        s *= 2
    g_last = jnp.concatenate(
        [jnp.broadcast_to(g[:, (n + 1) * c - 1:(n + 1) * c], (nh, c)) for n in range(nc)], axis=1)
    rows = jnp.concatenate(
        [_sigmoid(gates_t[0:nh]), g, jnp.exp(g), jnp.exp(g_last - g), jnp.exp(g_last),
         jnp.zeros((LANES - 5 * nh, ts), F32)], axis=0)
    gt_ref[...] = g
    gs_ref[...] = rows.T
    vnew_ref[...] = jnp.zeros_like(vnew_ref)

    ri = lax.broadcasted_iota(jnp.int32, (c, ts), 0)
    ci = lax.broadcasted_iota(jnp.int32, (c, ts), 1)
    incl = ri >= (ci & (c - 1))
    strict = ri > (ci & (c - 1))
    eye = jnp.where(ri == (ci & (c - 1)), 1.0, 0.0)
    chunk_of_lane = ci // c
    cat_mask = [jnp.where(chunk_of_lane == n, 1.0, 0.0).astype(BF16) for n in range(nc)]
    wide_lane = lax.broadcasted_iota(jnp.int32, (A_DIM, ts), 1) // c
    wide_mask = [jnp.where(wide_lane == n, 1.0, 0.0).astype(BF16) for n in range(nc)]

    def block_diag(cat):
        return jnp.concatenate([cat * cat_mask[n] for n in range(nc)], axis=0)

    inv, power = {}, {}
    r_all, qdec = {}, {}
    for h in range(nh):
        hs = slice(h * A_DIM, (h + 1) * A_DIM)
        q, k, v = q_ref[0, :, hs], k_ref[0, :, hs], v_ref[0, :, hs]
        bc = gs_ref[:, _G_BETA + h:_G_BETA + h + 1]
        gc = gs_ref[:, _G_CUM + h:_G_CUM + h + 1]
        egc = gs_ref[:, _G_EXP + h:_G_EXP + h + 1]
        kdf = gs_ref[:, _G_KDEC + h:_G_KDEC + h + 1]
        g_row = gt_ref[h:h + 1, :]
        kb = k * bc
        kt = k.T.astype(BF16)
        k_rhs = jnp.concatenate([kt * wide_mask[n] for n in range(nc)], axis=0)
        lhs = jnp.concatenate(
            [jnp.concatenate([kb[n * c:(n + 1) * c], q[n * c:(n + 1) * c]], axis=0)
             for n in range(nc)], axis=1).astype(BF16)
        qk = _dot(lhs, k_rhs)
        gc_cat = jnp.broadcast_to(gc[(nc - 1) * c:nc * c], (c, ts))
        for n in range(nc - 2, -1, -1):
            gc_cat = jnp.where(chunk_of_lane == n,
                               jnp.broadcast_to(gc[n * c:(n + 1) * c], (c, ts)), gc_cat)
        decay = jnp.exp(jnp.where(incl, gc_cat - g_row, -jnp.inf))
        lower = jnp.where(strict, qk[:c] * decay, 0.0)
        attn = jnp.where(incl, qk[c:] * decay, 0.0).astype(BF16)
        kdt = (k * kdf).T.astype(BF16)
        for n in range(nc):
            pair = slice((n // 2) * 2 * c, (n // 2 + 1) * 2 * c)
            attn_ref[h, n] = (attn * cat_mask[n])[:, pair]
            kdt_ref[h, n] = (kdt * wide_mask[n])[:, pair]
        r_all[h] = jnp.concatenate([v * bc, kb * egc], axis=1).astype(BF16)
        qdec[h] = (q * egc).astype(BF16)
        inv[h] = eye - lower
        power[h] = lower.astype(BF16)
    diag = {h: block_diag(power[h]) for h in range(nh)}
    p = 2
    while p < c:
        for h in range(nh):
            power[h] = _dot(power[h], diag[h]).astype(BF16)
            diag[h] = block_diag(power[h])
            inv[h] = inv[h] + _dot(inv[h].astype(BF16), diag[h])
        p *= 2
    for h in range(nh):
        uw = _dot(block_diag(inv[h].astype(BF16)), r_all[h])
        u_ref[h] = uw[:, :A_DIM]
        w = uw[:, A_DIM:].astype(BF16)
        for n in range(nc):
            sl = slice(n * c, (n + 1) * c)
            lhs_ref[h, n] = jnp.concatenate([w[sl], qdec[h][sl]], axis=0)

    onorm = onorm_ref[...]
    for n in range(nc):
        sl = slice(n * c, (n + 1) * c)
        pair = slice((n // 2) * 2 * c, (n // 2 + 1) * 2 * c)
        ws = [_dot(lhs_ref[h, n], state_ref[h].astype(BF16)) for h in range(nh)]
        for h in range(nh):
            hs = slice(h * A_DIM, (h + 1) * A_DIM)
            vnew_ref[h, sl, :] = (u_ref[h, sl, :] - ws[h][:c]).astype(BF16)
            v_pair = vnew_ref[h, pair, :]
            o = ws[h][c:] + _dot(attn_ref[h, n], v_pair)
            chunk_decay = gs_ref[n * c:n * c + 1, _G_LAST + h:_G_LAST + h + 1]
            state_ref[h] = state_ref[h] * chunk_decay + _dot(kdt_ref[h, n], v_pair)
            o = o * lax.rsqrt(jnp.mean(o * o, axis=-1, keepdims=True) + NORM_EPS)
            o_ref[0, sl, hs] = (o * onorm * z_ref[0, sl, hs]).astype(o_ref.dtype)


def _gated_deltanet(q, k, v, z, gates, a_log, dt_bias, out_norm):
    bsz, seq, _ = q.shape
    ts = GDN_TS
    nh = A_HEADS
    nc = ts // A_CHUNK
    act = pl.BlockSpec((1, ts, A_QK), lambda b, s: (b, s, 0))
    col = pl.BlockSpec((nh, 1), lambda b, s: (0, 0))
    return pl.pallas_call(
        _gdn_kernel,
        grid=(bsz, seq // ts),
        in_specs=[act, act, act, act, pl.BlockSpec((1, ts, LANES), lambda b, s: (b, s, 0)),
                  col, col, pl.BlockSpec((1, LANES), lambda b, s: (0, 0))],
        out_specs=act,
        out_shape=jax.ShapeDtypeStruct((bsz, seq, A_QK), BF16),
        scratch_shapes=[pltpu.VMEM((nh, A_DIM, A_DIM), F32),
                        pltpu.VMEM((nh, ts), F32),
                        pltpu.VMEM((ts, LANES), F32),
                        pltpu.VMEM((nh, nc, 2 * A_CHUNK, A_DIM), BF16),
                        pltpu.VMEM((nh, ts, A_DIM), F32),
                        pltpu.VMEM((nh, nc, A_CHUNK, 2 * A_CHUNK), BF16),
                        pltpu.VMEM((nh, nc, A_DIM, 2 * A_CHUNK), BF16),
                        pltpu.VMEM((nh, ts, A_DIM), BF16)],
        compiler_params=pltpu.CompilerParams(
            dimension_semantics=("parallel", "arbitrary"), vmem_limit_bytes=VMEM_LIMIT),
        name="gated_delta_rule",
    )(q, k, v, z, gates, a_log.reshape(nh, 1), dt_bias.reshape(nh, 1),
      out_norm.reshape(1, LANES))


def _attend(q_fn, kv, bias, nrows):
    dh = B_HEAD_DIM
    lane = lax.broadcasted_iota(jnp.int32, (nrows, LANES), 1)
    m_tile = jnp.zeros((nrows, LANES), F32)
    d_tile = jnp.ones((nrows, LANES), F32)
    pvs = []
    for g in range(B_KV_HEADS):
        qs = jnp.concatenate([q_fn(g * B_REP + r) for r in range(B_REP)], axis=0)
        s = _dot_nt(qs, kv[:, g * dh:(g + 1) * dh]) + bias
        m = jnp.max(s, axis=-1, keepdims=True)
        e = jnp.exp2((s - m) * ATTN_C2)
        den = jnp.sum(e, axis=-1, keepdims=True)
        pv = _dot(e.astype(BF16), kv[:, B_KW + g * dh:B_KW + (g + 1) * dh])
        for r in range(B_REP):
            rows = slice(r * nrows, (r + 1) * nrows)
            m_tile = jnp.where(lane == g * B_REP + r, m[rows], m_tile)
            d_tile = jnp.where(lane == g * B_REP + r, den[rows], d_tile)
            pvs.append(pv[rows])
    return m_tile, d_tile, pvs


def _attn_kernel(q0_ref, q1_ref, q2_ref, kv0_ref, kv1_ref, kv2_ref, bias_ref, bias2_ref,
                 o_ref, pv_ref, m_ref, l_ref):
    t = pl.program_id(1)
    blk = B_BLK
    dh = B_HEAD_DIM
    tq = o_ref.shape[0]

    def head(ref, r, rows):
        return lambda hd: ref[0, r, rows, hd * dh:(hd + 1) * dh]

    def window(ref, r, first_row):
        cur = pl.multiple_of(first_row, blk)
        prev = pl.multiple_of(jnp.maximum(first_row - blk, 0), blk)
        return jnp.concatenate([ref[0, r, pl.ds(prev, blk), :], ref[0, r, pl.ds(cur, blk), :]],
                               axis=0)

    def keep(g, idx, part):
        m_tile, d_tile, pvs = part
        m_ref[g, idx, :] = m_tile
        l_ref[g, idx, :] = d_tile
        for hd, pv in enumerate(pvs):
            pv_ref[g, hd, idx, :] = pv

    for i in range(tq // blk):
        rows = slice(i * blk, (i + 1) * blk)
        bias = bias_ref[jnp.minimum(t, 1)] if i == 0 else bias_ref[1]
        keep(0, rows, _attend(head(q0_ref, 0, rows), window(kv0_ref, 0, t * tq + i * blk),
                              bias, blk))
    d1 = q1_ref.shape[1]
    n1 = tq // d1
    for r in range(d1):
        keep(1, pl.ds(r, n1, stride=d1),
             _attend(head(q1_ref, r, slice(None)), window(kv1_ref, r, t * n1),
                     bias_ref[jnp.minimum(t, 1)], n1))
    d2 = q2_ref.shape[1]
    n2 = tq // d2
    bias2 = bias2_ref[t]
    for r in range(d2):
        keep(2, pl.ds(r, n2, stride=d2),
             _attend(head(q2_ref, r, slice(None)), kv2_ref[0, r], bias2, n2))

    ms = [m_ref[g] for g in range(B_N_GROUPS)]
    m_all = functools.reduce(jnp.maximum, ms)
    ws = [jnp.exp2((m - m_all) * ATTN_C2) for m in ms]
    inv_l = 1.0 / functools.reduce(jnp.add, [w * l_ref[g] for g, w in enumerate(ws)])
    ws = [w * inv_l for w in ws]
    for hd in range(B_HEADS):
        o = ws[0][:, hd:hd + 1] * pv_ref[0, hd]
        for g in range(1, B_N_GROUPS):
            o = o + ws[g][:, hd:hd + 1] * pv_ref[g, hd]
        o_ref[:, hd * dh:(hd + 1) * dh] = o.astype(o_ref.dtype)


def _band_bias(seq):
    qi = np.arange(B_REP * B_BLK)[:, None] & (B_BLK - 1)
    kj = np.arange(2 * B_BLK)[None, :]
    dist = qi + B_BLK - kj
    band = (dist >= 0) & (dist <= B_BLK)
    first = band & (kj >= B_BLK)
    bias = np.where(np.stack([first, band]), 0.0, -np.inf)
    dil = B_GROUPS[-1][1]
    n2 = ATTN_TQ // dil
    t = np.arange(seq // ATTN_TQ)[:, None, None]
    pos = t * n2 + (np.arange(B_REP * n2)[None, :, None] % n2)
    dist2 = pos - kj[None]
    bias2 = np.where((dist2 >= 0) & (dist2 <= B_BLK), 0.0, -np.inf)
    return jnp.asarray(bias, F32), jnp.asarray(bias2, F32)


def _dilated_attention(qs, kvs, biases, seq):
    bsz = qs[0].shape[0]
    tq = ATTN_TQ
    assert seq // B_GROUPS[-1][1] == 2 * B_BLK
    nt = seq // tq

    def q_spec(a):
        dil = a.shape[1]
        return pl.BlockSpec((1, dil, tq // dil, B_OUT), lambda b, t: (b, 0, t, 0))

    def kv_spec(a):
        return pl.BlockSpec((1,) + a.shape[1:], lambda b, t: (b, 0, 0, 0))

    def whole(a):
        return pl.BlockSpec(a.shape, lambda b, t: (0, 0, 0))

    return pl.pallas_call(
        _attn_kernel,
        grid=(bsz, nt),
        in_specs=[q_spec(a) for a in qs] + [kv_spec(a) for a in kvs] + [whole(a) for a in biases],
        out_specs=pl.BlockSpec((tq, B_OUT), lambda b, t: (b * nt + t, 0)),
        out_shape=jax.ShapeDtypeStruct((bsz * seq, B_OUT), BF16),
        scratch_shapes=[pltpu.VMEM((B_N_GROUPS, B_HEADS, tq, B_HEAD_DIM), F32),
                        pltpu.VMEM((B_N_GROUPS, tq, LANES), F32),
                        pltpu.VMEM((B_N_GROUPS, tq, LANES), F32)],
        compiler_params=pltpu.CompilerParams(
            dimension_semantics=("parallel", "arbitrary"), vmem_limit_bytes=VMEM_LIMIT),
        name="dilated_attention",
    )(*qs, *kvs, *biases)


def _tail_kernel(x_ref, mix_ref, p_ref, wo_ref, gm_ref, wup_ref, wdn_ref, wg_ref, wp_ref, gf_ref,
                 out_ref, *, final):
    x1 = x_ref[...] + _dot(mix_ref[...], wo_ref[...])
    h = _rms(x1, gm_ref[...]).astype(BF16)
    acc = x1
    for f in range(D_FF // TAIL_TF):
        fs = slice(f * TAIL_TF, (f + 1) * TAIL_TF)
        u = jnp.maximum(_dot(h, wup_ref[:, fs]), 0.0)
        acc = acc + _dot((u * u).astype(BF16), wdn_ref[fs, :])
    gate = _sigmoid(_dot(acc.astype(BF16), wg_ref[...]))
    x3 = acc + gate * _dot(p_ref[...].astype(BF16), wp_ref[...])
    if final:
        x3 = _rms(x3, gf_ref[...])
    out_ref[...] = x3


def _layer_tail(x, mix, p, layer, w_o, g_mlp, w_up, w_dn, w_gate, w_ple, g_final, *, final):
    m, d = x.shape
    tm = TAIL_TM

    def rows(width):
        return pl.BlockSpec((tm, width), lambda i: (i, 0))

    def resident(a):
        return pl.BlockSpec(a.shape, lambda i: (0, 0), pipeline_mode=pl.Buffered(1))

    weights = (w_o, g_mlp.reshape(1, d), w_up, w_dn, w_gate, w_ple, g_final.reshape(1, d))
    return pl.pallas_call(
        functools.partial(_tail_kernel, final=final),
        grid=(m // tm,),
        in_specs=[rows(d), rows(d), pl.BlockSpec((None, tm, PLE_DIM), lambda i: (layer, i, 0))]
        + [resident(w) for w in weights],
        out_specs=rows(d),
        out_shape=jax.ShapeDtypeStruct((m, d), F32),
        compiler_params=pltpu.CompilerParams(
            dimension_semantics=("parallel",), vmem_limit_bytes=VMEM_LIMIT),
        name="layer_tail",
    )(x, mix, p, *weights)


def _rope_tables(seq):
    inv = np.power(np.float32(ROPE_THETA), -np.arange(0, ROPE_DIM, 2, dtype=np.float32) / ROPE_DIM)
    ang = np.arange(seq, dtype=np.float32)[:, None] * inv[None, :].astype(np.float32)
    cos, sin = np.cos(ang), np.sin(ang)
    c = np.ones((seq, LANES), np.float32)
    s_lo = np.zeros((seq, LANES), np.float32)
    s_hi = np.zeros((seq, LANES), np.float32)
    c[:, :ROPE_HALF] = cos
    c[:, ROPE_HALF:ROPE_DIM] = cos
    s_lo[:, :ROPE_HALF] = -sin
    s_hi[:, ROPE_HALF:ROPE_DIM] = sin
    return jnp.asarray(c), jnp.asarray(s_lo), jnp.asarray(s_hi)


def _group_kv_columns(w_kv):
    d = w_kv.shape[0]
    w = w_kv.reshape(d, 2, B_N_GROUPS, B_KW)
    return jnp.transpose(w, (0, 2, 1, 3)).reshape(d, 2 * B_N_GROUPS * B_KW)


def kernel(x, p, attn_norm, mlp_norm, a_w_in, a_conv_w, a_log, a_dt_bias, a_out_norm, a_w_out,
           kv_norm, b_w_kv, b_w_q, b_w_o, mlp_w_up, mlp_w_down, ple_w_proj, ple_w_gate, final_norm):
    bsz, seq, d = x.shape
    m = bsz * seq
    xf = x.reshape(m, d)
    rope = _rope_tables(seq)
    biases = _band_bias(seq)
    kvs = None
    for i in range(DEPTH):
        if i < N_A_LAYERS:
            w_in = jnp.pad(a_w_in[i], ((0, 0), (0, A_IN_PAD - A_IN_COLS))).astype(BF16)
            q, k, v, z, gates = _a_projection(xf, seq, attn_norm[i], w_in, a_conv_w[i])
            mix = _gated_deltanet(*(t.reshape(bsz, seq, -1) for t in (q, k, v, z, gates)),
                                  a_log[i], a_dt_bias[i], a_out_norm[i]).reshape(m, A_QK)
            w_o = a_w_out[i]
        else:
            j = i - N_A_LAYERS
            w_q = b_w_q[j].astype(BF16)
            if kvs is None:
                outs = _b_projection(xf, bsz, seq, rope, attn_norm[i], w_q, kv_norm,
                                     _group_kv_columns(b_w_kv).astype(BF16))
                qs, kvs = outs[:B_N_GROUPS], outs[B_N_GROUPS:]
            else:
                qs = _b_projection(xf, bsz, seq, rope, attn_norm[i], w_q)
            mix = _dilated_attention(qs, kvs, biases, seq)
            w_o = b_w_o[j]
        xf = _layer_tail(xf, mix, p.reshape(DEPTH, m, PLE_DIM), i, w_o.astype(BF16),
                         mlp_norm[i], mlp_w_up[i].astype(BF16), mlp_w_down[i].astype(BF16),
                         ple_w_gate[i].astype(BF16), ple_w_proj[i].astype(BF16), final_norm,
                         final=(i == DEPTH - 1))
    return xf.reshape(bsz, seq, d)
```

```python
import functools

import jax
import jax.numpy as jnp
import numpy as np
from jax import lax
from jax.experimental import pallas as pl
from jax.experimental.pallas import tpu as pltpu

F32 = jnp.float32
BF16 = jnp.bfloat16

D_MODEL = 1024
DEPTH = 4
N_A_LAYERS = DEPTH // 2
PLE_DIM = 256
D_FF = 4 * D_MODEL
NORM_EPS = 1e-6

A_HEADS = 8
A_DIM = 128
A_CONV = 4
A_CHUNK = 64
A_QK = A_HEADS * A_DIM
A_IN_COLS = 4 * A_QK + 2 * A_HEADS
LANES = 128
A_IN_PAD = 4 * A_QK + LANES

B_HEAD_DIM = 128
B_GROUPS = ((128, 1), (512, 4), (2048, 16))
B_N_GROUPS = len(B_GROUPS)
B_HEADS = 8
B_KV_HEADS = 2
B_REP = B_HEADS // B_KV_HEADS
B_BLK = 128
B_OUT = B_HEADS * B_HEAD_DIM
B_KW = B_KV_HEADS * B_HEAD_DIM
ROPE_THETA = 500000.0
ROPE_DIM = B_HEAD_DIM // 4
ROPE_HALF = ROPE_DIM // 2

VMEM_LIMIT = 56 * 1024 * 1024

PROJ_TM = 512
A_PROJ_TN = 256
GDN_TS = 256
ATTN_TQ = 512
ATTN_C2 = (B_HEAD_DIM ** -0.5) * 1.4426950408889634
TAIL_TM = 512
TAIL_TF = 512


def _dot(a, b):
    return jnp.dot(a, b, preferred_element_type=F32)


def _dot_nt(a, b):
    return lax.dot_general(a, b, (((1,), (1,)), ((), ())), preferred_element_type=F32)


def _rms(x, gain):
    return x * lax.rsqrt(jnp.mean(x * x, axis=-1, keepdims=True) + NORM_EPS) * gain


def _sigmoid(x):
    return 1.0 / (1.0 + jnp.exp(-x))


def _conv_silu(y, hist_ref, prev8, w):
    tm = y.shape[0]
    hist_ref[0:8, :] = prev8
    hist_ref[8:8 + tm, :] = y
    acc = y * w[A_CONV - 1:A_CONV, :]
    for s in range(1, A_CONV):
        acc = acc + hist_ref[8 - s:8 - s + tm, :] * w[A_CONV - 1 - s:A_CONV - s, :]
    return acc * _sigmoid(acc)


def _l2(x):
    return x * lax.rsqrt(jnp.sum(x * x, axis=-1, keepdims=True) + NORM_EPS)


def _aproj_kernel(x_ref, g_ref, w_ref, cw_ref, q_ref, k_ref, v_ref, z_ref, gate_ref,
                  carry_ref, hist_ref, *, sblk):
    tm = x_ref.shape[0]
    wd = A_PROJ_TN

    @pl.when(pl.program_id(0) % sblk == 0)
    def _():
        carry_ref[...] = jnp.zeros_like(carry_ref)

    h = _rms(x_ref[...], g_ref[...]).astype(BF16)
    for t, out_ref in enumerate((q_ref, k_ref, v_ref)):
        for j in range(A_QK // wd):
            cs = slice(j * wd, (j + 1) * wd)
            ws = slice(t * A_QK + j * wd, t * A_QK + (j + 1) * wd)
            y = _dot(h, w_ref[:, ws])
            act = _conv_silu(y, hist_ref, carry_ref[t, :, cs], cw_ref[:, ws])
            carry_ref[t, :, cs] = y[tm - 8:tm, :]
            for hd in range(wd // A_DIM):
                hs = slice(hd * A_DIM, (hd + 1) * A_DIM)
                piece = act[:, hs]
                if t == 0:
                    piece = _l2(piece) * (A_DIM ** -0.5)
                elif t == 1:
                    piece = _l2(piece)
                out_ref[:, j * wd + hd * A_DIM:j * wd + (hd + 1) * A_DIM] = piece
    for j in range(A_QK // wd):
        cs = slice(j * wd, (j + 1) * wd)
        z = _dot(h, w_ref[:, 3 * A_QK + j * wd:3 * A_QK + (j + 1) * wd])
        z_ref[:, cs] = z * _sigmoid(z)
    gate_ref[...] = _dot(h, w_ref[:, 4 * A_QK:])


def _a_projection(x, seq, gain, w_in, conv_w):
    m, d = x.shape
    tm = PROJ_TM
    rows = pl.BlockSpec((tm, A_QK), lambda i: (i, 0))
    act = jax.ShapeDtypeStruct((m, A_QK), F32)
    return pl.pallas_call(
        functools.partial(_aproj_kernel, sblk=seq // tm),
        grid=(m // tm,),
        in_specs=[pl.BlockSpec((tm, d), lambda i: (i, 0)),
                  pl.BlockSpec((1, d), lambda i: (0, 0)),
                  pl.BlockSpec(w_in.shape, lambda i: (0, 0)),
                  pl.BlockSpec(conv_w.shape, lambda i: (0, 0))],
        out_specs=[rows, rows, rows, rows, pl.BlockSpec((tm, LANES), lambda i: (i, 0))],
        out_shape=[act, act, act, act, jax.ShapeDtypeStruct((m, LANES), F32)],
        scratch_shapes=[pltpu.VMEM((3, 8, A_QK), F32),
                        pltpu.VMEM((tm + 8, A_PROJ_TN), F32)],
        compiler_params=pltpu.CompilerParams(
            dimension_semantics=("arbitrary",), vmem_limit_bytes=VMEM_LIMIT),
        name="gdn_projection",
    )(x, gain.reshape(1, d), w_in, conv_w)


def _rope(y, c, s_lo, s_hi):
    return (y * c + pltpu.roll(y, LANES - ROPE_HALF, axis=1) * s_lo
            + pltpu.roll(y, ROPE_HALF, axis=1) * s_hi)


def _store_residue_major(y, n_rope, tabs, dilation, out_ref, scr_ref):
    tm, w = y.shape
    for b in range(w // LANES):
        sl = slice(b * LANES, (b + 1) * LANES)
        blk = _rope(y[:, sl], *tabs) if b < n_rope else y[:, sl]
        if dilation == 1:
            out_ref[0, 0, :, sl] = blk.astype(out_ref.dtype)
        else:
            scr_ref[b] = blk
    if dilation > 1:
        rows = tm // dilation
        for r in range(dilation):
            for b in range(w // LANES):
                out_ref[0, r, :, b * LANES:(b + 1) * LANES] = scr_ref[
                    b, pl.ds(r, rows, stride=dilation), :].astype(out_ref.dtype)


def _bproj_kernel(*refs, with_kv):
    if with_kv:
        (x_ref, gq_ref, wq_ref, gkv_ref, wkv_ref, c_ref, slo_ref, shi_ref,
         q0_ref, q1_ref, q2_ref, kv0_ref, kv1_ref, kv2_ref, scr_ref) = refs
    else:
        (x_ref, gq_ref, wq_ref, c_ref, slo_ref, shi_ref,
         q0_ref, q1_ref, q2_ref, scr_ref) = refs
    x = x_ref[...]
    xn = x * lax.rsqrt(jnp.mean(x * x, axis=-1, keepdims=True) + NORM_EPS)
    tabs = (c_ref[...], slo_ref[...], shi_ref[...])
    hq = (xn * gq_ref[...]).astype(BF16)
    for g, (q_ref, (_, dil)) in enumerate(zip((q0_ref, q1_ref, q2_ref), B_GROUPS)):
        y = _dot(hq, wq_ref[:, g * B_OUT:(g + 1) * B_OUT])
        _store_residue_major(y, B_HEADS, tabs, dil, q_ref, scr_ref)
    if with_kv:
        hkv = (xn * gkv_ref[...]).astype(BF16)
        for g, (kv_ref, (_, dil)) in enumerate(zip((kv0_ref, kv1_ref, kv2_ref), B_GROUPS)):
            y = _dot(hkv, wkv_ref[:, g * 2 * B_KW:(g + 1) * 2 * B_KW])
            _store_residue_major(y, B_KV_HEADS, tabs, dil, kv_ref, scr_ref)


def _b_projection(x, bsz, seq, rope, g_q, w_q, g_kv=None, w_kv=None):
    m, d = x.shape
    tm = PROJ_TM
    sblk = seq // tm
    with_kv = w_kv is not None

    def whole(a):
        return pl.BlockSpec(a.shape, lambda i: (0, 0))

    def out(width, dil):
        return pl.BlockSpec((1, dil, tm // dil, width), lambda i: (i // sblk, 0, i % sblk, 0))

    def out_shape(width, dil):
        return jax.ShapeDtypeStruct((bsz, dil, seq // dil, width), BF16)

    tab = pl.BlockSpec((tm, LANES), lambda i: (i % sblk, 0))
    args = [x, g_q.reshape(1, d), w_q]
    in_specs = [pl.BlockSpec((tm, d), lambda i: (i, 0)), whole(args[1]), whole(w_q)]
    if with_kv:
        args += [g_kv.reshape(1, d), w_kv]
        in_specs += [whole(args[3]), whole(w_kv)]
    args += list(rope)
    in_specs += [tab] * len(rope)
    widths = [B_OUT] * B_N_GROUPS + ([2 * B_KW] * B_N_GROUPS if with_kv else [])
    dils = [dil for _, dil in B_GROUPS] * (2 if with_kv else 1)
    return pl.pallas_call(
        functools.partial(_bproj_kernel, with_kv=with_kv),
        grid=(m // tm,),
        in_specs=in_specs,
        out_specs=[out(w, dl) for w, dl in zip(widths, dils)],
        out_shape=[out_shape(w, dl) for w, dl in zip(widths, dils)],
        scratch_shapes=[pltpu.VMEM((B_HEADS, tm, LANES), F32)],
        compiler_params=pltpu.CompilerParams(
            dimension_semantics=("parallel",), vmem_limit_bytes=VMEM_LIMIT),
        name="attn_projection",
    )(*args)


_G_BETA, _G_CUM, _G_EXP, _G_KDEC, _G_LAST = (i * A_HEADS for i in range(5))


def _gdn_kernel(q_ref, k_ref, v_ref, z_ref, gate_ref, alog_ref, dt_ref, onorm_ref,
                o_ref, state_ref, gt_ref, gs_ref, lhs_ref, u_ref, attn_ref, kdt_ref, vnew_ref):
    ts = q_ref.shape[1]
    c = A_CHUNK
    nc = ts // c
    nh = A_HEADS

    @pl.when(pl.program_id(1) == 0)
    def _():
        state_ref[...] = jnp.zeros_like(state_ref)

    gates_t = gate_ref[0].T
    a = gates_t[nh:2 * nh] + dt_ref[...]
    softplus = jnp.maximum(a, 0.0) + jnp.log(1.0 + jnp.exp(-jnp.abs(a)))
    g = -jnp.exp(alog_ref[...]) * softplus
    lane_in_chunk = lax.broadcasted_iota(jnp.int32, (nh, ts), 1) & (c - 1)
    s = 1
    while s < c:
        g = g + jnp.where(lane_in_chunk >= s, pltpu.roll(g, s, axis=1), 0.0)
        s *= 2
    g_last = jnp.concatenate(
        [jnp.broadcast_to(g[:, (n + 1) * c - 1:(n + 1) * c], (nh, c)) for n in range(nc)], axis=1)
    rows = jnp.concatenate(
        [_sigmoid(gates_t[0:nh]), g, jnp.exp(g), jnp.exp(g_last - g), jnp.exp(g_last),
         jnp.zeros((LANES - 5 * nh, ts), F32)], axis=0)
    gt_ref[...] = g
    gs_ref[...] = rows.T
    vnew_ref[...] = jnp.zeros_like(vnew_ref)

    ri = lax.broadcasted_iota(jnp.int32, (c, ts), 0)
    ci = lax.broadcasted_iota(jnp.int32, (c, ts), 1)
    incl = ri >= (ci & (c - 1))
    strict = ri > (ci & (c - 1))
    eye = jnp.where(ri == (ci & (c - 1)), 1.0, 0.0)
    chunk_of_lane = ci // c
    cat_mask = [jnp.where(chunk_of_lane == n, 1.0, 0.0).astype(BF16) for n in range(nc)]
    wide_lane = lax.broadcasted_iota(jnp.int32, (A_DIM, ts), 1) // c
    wide_mask = [jnp.where(wide_lane == n, 1.0, 0.0).astype(BF16) for n in range(nc)]

    def block_diag(cat):
        return jnp.concatenate([cat * cat_mask[n] for n in range(nc)], axis=0)

    inv, power = {}, {}
    r_all, qdec = {}, {}
    for h in range(nh):
        hs = slice(h * A_DIM, (h + 1) * A_DIM)
        q, k, v = q_ref[0, :, hs], k_ref[0, :, hs], v_ref[0, :, hs]
        bc = gs_ref[:, _G_BETA + h:_G_BETA + h + 1]
        gc = gs_ref[:, _G_CUM + h:_G_CUM + h + 1]
        egc = gs_ref[:, _G_EXP + h:_G_EXP + h + 1]
        kdf = gs_ref[:, _G_KDEC + h:_G_KDEC + h + 1]
        g_row = gt_ref[h:h + 1, :]
        kb = k * bc
        kt = k.T.astype(BF16)
        k_rhs = jnp.concatenate([kt * wide_mask[n] for n in range(nc)], axis=0)
        lhs = jnp.concatenate(
            [jnp.concatenate([kb[n * c:(n + 1) * c], q[n * c:(n + 1) * c]], axis=0)
             for n in range(nc)], axis=1).astype(BF16)
        qk = _dot(lhs, k_rhs)
        gc_cat = jnp.broadcast_to(gc[(nc - 1) * c:nc * c], (c, ts))
        for n in range(nc - 2, -1, -1):
            gc_cat = jnp.where(chunk_of_lane == n,
                               jnp.broadcast_to(gc[n * c:(n + 1) * c], (c, ts)), gc_cat)
        decay = jnp.exp(jnp.where(incl, gc_cat - g_row, -jnp.inf))
        lower = jnp.where(strict, qk[:c] * decay, 0.0)
        attn = jnp.where(incl, qk[c:] * decay, 0.0).astype(BF16)
        kdt = (k * kdf).T.astype(BF16)
        for n in range(nc):
            pair = slice((n // 2) * 2 * c, (n // 2 + 1) * 2 * c)
            attn_ref[h, n] = (attn * cat_mask[n])[:, pair]
            kdt_ref[h, n] = (kdt * wide_mask[n])[:, pair]
        r_all[h] = jnp.concatenate([v * bc, kb * egc], axis=1).astype(BF16)
        qdec[h] = (q * egc).astype(BF16)
        inv[h] = eye - lower
        power[h] = lower.astype(BF16)
    diag = {h: block_diag(power[h]) for h in range(nh)}
    p = 2
    while p < c:
        for h in range(nh):
            power[h] = _dot(power[h], diag[h]).astype(BF16)
            diag[h] = block_diag(power[h])
            inv[h] = inv[h] + _dot(inv[h].astype(BF16), diag[h])
        p *= 2
    for h in range(nh):
        uw = _dot(block_diag(inv[h].astype(BF16)), r_all[h])
        u_ref[h] = uw[:, :A_DIM]
        w = uw[:, A_DIM:].astype(BF16)
        for n in range(nc):
            sl = slice(n * c, (n + 1) * c)
            lhs_ref[h, n] = jnp.concatenate([w[sl], qdec[h][sl]], axis=0)

    onorm = onorm_ref[...]
    for n in range(nc):
        sl = slice(n * c, (n + 1) * c)
        pair = slice((n // 2) * 2 * c, (n // 2 + 1) * 2 * c)
        ws = [_dot(lhs_ref[h, n], state_ref[h].astype(BF16)) for h in range(nh)]
        for h in range(nh):
            hs = slice(h * A_DIM, (h + 1) * A_DIM)
            vnew_ref[h, sl, :] = (u_ref[h, sl, :] - ws[h][:c]).astype(BF16)
            v_pair = vnew_ref[h, pair, :]
            o = ws[h][c:] + _dot(attn_ref[h, n], v_pair)
            chunk_decay = gs_ref[n * c:n * c + 1, _G_LAST + h:_G_LAST + h + 1]
            state_ref[h] = state_ref[h] * chunk_decay + _dot(kdt_ref[h, n], v_pair)
            o = o * lax.rsqrt(jnp.mean(o * o, axis=-1, keepdims=True) + NORM_EPS)
            o_ref[0, sl, hs] = (o * onorm * z_ref[0, sl, hs]).astype(o_ref.dtype)


def _gated_deltanet(q, k, v, z, gates, a_log, dt_bias, out_norm):
    bsz, seq, _ = q.shape
    ts = GDN_TS
    nh = A_HEADS
    nc = ts // A_CHUNK
    act = pl.BlockSpec((1, ts, A_QK), lambda b, s: (b, s, 0))
    col = pl.BlockSpec((nh, 1), lambda b, s: (0, 0))
    return pl.pallas_call(
        _gdn_kernel,
        grid=(bsz, seq // ts),
        in_specs=[act, act, act, act, pl.BlockSpec((1, ts, LANES), lambda b, s: (b, s, 0)),
                  col, col, pl.BlockSpec((1, LANES), lambda b, s: (0, 0))],
        out_specs=act,
        out_shape=jax.ShapeDtypeStruct((bsz, seq, A_QK), BF16),
        scratch_shapes=[pltpu.VMEM((nh, A_DIM, A_DIM), F32),
                        pltpu.VMEM((nh, ts), F32),
                        pltpu.VMEM((ts, LANES), F32),
                        pltpu.VMEM((nh, nc, 2 * A_CHUNK, A_DIM), BF16),
                        pltpu.VMEM((nh, ts, A_DIM), F32),
                        pltpu.VMEM((nh, nc, A_CHUNK, 2 * A_CHUNK), BF16),
                        pltpu.VMEM((nh, nc, A_DIM, 2 * A_CHUNK), BF16),
                        pltpu.VMEM((nh, ts, A_DIM), BF16)],
        compiler_params=pltpu.CompilerParams(
            dimension_semantics=("parallel", "arbitrary"), vmem_limit_bytes=VMEM_LIMIT),
        name="gated_delta_rule",
    )(q, k, v, z, gates, a_log.reshape(nh, 1), dt_bias.reshape(nh, 1),
      out_norm.reshape(1, LANES))


def _attend(q_fn, kv, bias, nrows):
    dh = B_HEAD_DIM
    lane = lax.broadcasted_iota(jnp.int32, (nrows, LANES), 1)
    m_tile = jnp.zeros((nrows, LANES), F32)
    d_tile = jnp.ones((nrows, LANES), F32)
    pvs = []
    for g in range(B_KV_HEADS):
        qs = jnp.concatenate([q_fn(g * B_REP + r) for r in range(B_REP)], axis=0)
        s = _dot_nt(qs, kv[:, g * dh:(g + 1) * dh]) + bias
        m = jnp.max(s, axis=-1, keepdims=True)
        e = jnp.exp2((s - m) * ATTN_C2)
        den = jnp.sum(e, axis=-1, keepdims=True)
        pv = _dot(e.astype(BF16), kv[:, B_KW + g * dh:B_KW + (g + 1) * dh])
        for r in range(B_REP):
            rows = slice(r * nrows, (r + 1) * nrows)
            m_tile = jnp.where(lane == g * B_REP + r, m[rows], m_tile)
            d_tile = jnp.where(lane == g * B_REP + r, den[rows], d_tile)
            pvs.append(pv[rows])
    return m_tile, d_tile, pvs


def _attn_kernel(q0_ref, q1_ref, q2_ref, kv0_ref, kv1_ref, kv2_ref, bias_ref, bias2_ref,
                 o_ref, pv_ref, m_ref, l_ref):
    t = pl.program_id(1)
    blk = B_BLK
    dh = B_HEAD_DIM
    tq = o_ref.shape[0]

    def head(ref, r, rows):
        return lambda hd: ref[0, r, rows, hd * dh:(hd + 1) * dh]

    def window(ref, r, first_row):
        cur = pl.multiple_of(first_row, blk)
        prev = pl.multiple_of(jnp.maximum(first_row - blk, 0), blk)
        return jnp.concatenate([ref[0, r, pl.ds(prev, blk), :], ref[0, r, pl.ds(cur, blk), :]],
                               axis=0)

    def keep(g, idx, part):
        m_tile, d_tile, pvs = part
        m_ref[g, idx, :] = m_tile
        l_ref[g, idx, :] = d_tile
        for hd, pv in enumerate(pvs):
            pv_ref[g, hd, idx, :] = pv

    for i in range(tq // blk):
        rows = slice(i * blk, (i + 1) * blk)
        bias = bias_ref[jnp.minimum(t, 1)] if i == 0 else bias_ref[1]
        keep(0, rows, _attend(head(q0_ref, 0, rows), window(kv0_ref, 0, t * tq + i * blk),
                              bias, blk))
    d1 = q1_ref.shape[1]
    n1 = tq // d1
    for r in range(d1):
        keep(1, pl.ds(r, n1, stride=d1),
             _attend(head(q1_ref, r, slice(None)), window(kv1_ref, r, t * n1),
                     bias_ref[jnp.minimum(t, 1)], n1))
    d2 = q2_ref.shape[1]
    n2 = tq // d2
    bias2 = bias2_ref[t]
    for r in range(d2):
        keep(2, pl.ds(r, n2, stride=d2),
             _attend(head(q2_ref, r, slice(None)), kv2_ref[0, r], bias2, n2))

    ms = [m_ref[g] for g in range(B_N_GROUPS)]
    m_all = functools.reduce(jnp.maximum, ms)
    ws = [jnp.exp2((m - m_all) * ATTN_C2) for m in ms]
    inv_l = 1.0 / functools.reduce(jnp.add, [w * l_ref[g] for g, w in enumerate(ws)])
    ws = [w * inv_l for w in ws]
    for hd in range(B_HEADS):
        o = ws[0][:, hd:hd + 1] * pv_ref[0, hd]
        for g in range(1, B_N_GROUPS):
            o = o + ws[g][:, hd:hd + 1] * pv_ref[g, hd]
        o_ref[:, hd * dh:(hd + 1) * dh] = o.astype(o_ref.dtype)


def _band_bias(seq):
    qi = np.arange(B_REP * B_BLK)[:, None] & (B_BLK - 1)
    kj = np.arange(2 * B_BLK)[None, :]
    dist = qi + B_BLK - kj
    band = (dist >= 0) & (dist <= B_BLK)
    first = band & (kj >= B_BLK)
    bias = np.where(np.stack([first, band]), 0.0, -np.inf)
    dil = B_GROUPS[-1][1]
    n2 = ATTN_TQ // dil
    t = np.arange(seq // ATTN_TQ)[:, None, None]
    pos = t * n2 + (np.arange(B_REP * n2)[None, :, None] % n2)
    dist2 = pos - kj[None]
    bias2 = np.where((dist2 >= 0) & (dist2 <= B_BLK), 0.0, -np.inf)
    return jnp.asarray(bias, F32), jnp.asarray(bias2, F32)


def _dilated_attention(qs, kvs, biases, seq):
    bsz = qs[0].shape[0]
    tq = ATTN_TQ
    assert seq // B_GROUPS[-1][1] == 2 * B_BLK
    nt = seq // tq

    def q_spec(a):
        dil = a.shape[1]
        return pl.BlockSpec((1, dil, tq // dil, B_OUT), lambda b, t: (b, 0, t, 0))

    def kv_spec(a):
        return pl.BlockSpec((1,) + a.shape[1:], lambda b, t: (b, 0, 0, 0))

    def whole(a):
        return pl.BlockSpec(a.shape, lambda b, t: (0, 0, 0))

    return pl.pallas_call(
        _attn_kernel,
        grid=(bsz, nt),
        in_specs=[q_spec(a) for a in qs] + [kv_spec(a) for a in kvs] + [whole(a) for a in biases],
        out_specs=pl.BlockSpec((tq, B_OUT), lambda b, t: (b * nt + t, 0)),
        out_shape=jax.ShapeDtypeStruct((bsz * seq, B_OUT), BF16),
        scratch_shapes=[pltpu.VMEM((B_N_GROUPS, B_HEADS, tq, B_HEAD_DIM), F32),
                        pltpu.VMEM((B_N_GROUPS, tq, LANES), F32),
                        pltpu.VMEM((B_N_GROUPS, tq, LANES), F32)],
        compiler_params=pltpu.CompilerParams(
            dimension_semantics=("parallel", "arbitrary"), vmem_limit_bytes=VMEM_LIMIT),
        name="dilated_attention",
    )(*qs, *kvs, *biases)


def _tail_kernel(x_ref, mix_ref, p_ref, wo_ref, gm_ref, wup_ref, wdn_ref, wg_ref, wp_ref, gf_ref,
                 out_ref, *, final):
    x1 = x_ref[...] + _dot(mix_ref[...], wo_ref[...])
    h = _rms(x1, gm_ref[...]).astype(BF16)
    acc = x1
    for f in range(D_FF // TAIL_TF):
        fs = slice(f * TAIL_TF, (f + 1) * TAIL_TF)
        u = jnp.maximum(_dot(h, wup_ref[:, fs]), 0.0)
        acc = acc + _dot((u * u).astype(BF16), wdn_ref[fs, :])
    gate = _sigmoid(_dot(acc.astype(BF16), wg_ref[...]))
    x3 = acc + gate * _dot(p_ref[...].astype(BF16), wp_ref[...])
    if final:
        x3 = _rms(x3, gf_ref[...])
    out_ref[...] = x3


def _layer_tail(x, mix, p, layer, w_o, g_mlp, w_up, w_dn, w_gate, w_ple, g_final, *, final):
    m, d = x.shape
    tm = TAIL_TM

    def rows(width):
        return pl.BlockSpec((tm, width), lambda i: (i, 0))

    def resident(a):
        return pl.BlockSpec(a.shape, lambda i: (0, 0), pipeline_mode=pl.Buffered(1))

    def resident_layer(a):
        return pl.BlockSpec((None,) + a.shape[1:], lambda i: (layer, 0, 0),
                            pipeline_mode=pl.Buffered(1))

    stacked = (w_up, w_dn, w_gate, w_ple)
    return pl.pallas_call(
        functools.partial(_tail_kernel, final=final),
        grid=(m // tm,),
        in_specs=[rows(d), rows(d), pl.BlockSpec((None, tm, PLE_DIM), lambda i: (layer, i, 0)),
                  resident(w_o), resident_layer(g_mlp)]
        + [resident_layer(w) for w in stacked] + [resident(g_final)],
        out_specs=rows(d),
        out_shape=jax.ShapeDtypeStruct((m, d), F32),
        compiler_params=pltpu.CompilerParams(
            dimension_semantics=("parallel",), vmem_limit_bytes=VMEM_LIMIT),
        name="layer_tail",
    )(x, mix, p, w_o, g_mlp, *stacked, g_final)


def _rope_tables(seq):
    inv = np.power(np.float32(ROPE_THETA), -np.arange(0, ROPE_DIM, 2, dtype=np.float32) / ROPE_DIM)
    ang = np.arange(seq, dtype=np.float32)[:, None] * inv[None, :].astype(np.float32)
    cos, sin = np.cos(ang), np.sin(ang)
    c = np.ones((seq, LANES), np.float32)
    s_lo = np.zeros((seq, LANES), np.float32)
    s_hi = np.zeros((seq, LANES), np.float32)
    c[:, :ROPE_HALF] = cos
    c[:, ROPE_HALF:ROPE_DIM] = cos
    s_lo[:, :ROPE_HALF] = -sin
    s_hi[:, ROPE_HALF:ROPE_DIM] = sin
    return jnp.asarray(c), jnp.asarray(s_lo), jnp.asarray(s_hi)


def _group_kv_columns(w_kv):
    d = w_kv.shape[0]
    w = w_kv.reshape(d, 2, B_N_GROUPS, B_KW)
    return jnp.transpose(w, (0, 2, 1, 3)).reshape(d, 2 * B_N_GROUPS * B_KW)


def kernel(x, p, attn_norm, mlp_norm, a_w_in, a_conv_w, a_log, a_dt_bias, a_out_norm, a_w_out,
           kv_norm, b_w_kv, b_w_q, b_w_o, mlp_w_up, mlp_w_down, ple_w_proj, ple_w_gate, final_norm):
    bsz, seq, d = x.shape
    m = bsz * seq
    xf = x.reshape(m, d)
    rope = _rope_tables(seq)
    biases = _band_bias(seq)
    p = p.reshape(DEPTH, m, PLE_DIM)
    g_mlp = mlp_norm.reshape(DEPTH, 1, d)
    g_final = final_norm.reshape(1, d)
    w_up, w_dn = mlp_w_up.astype(BF16), mlp_w_down.astype(BF16)
    w_gate, w_ple = ple_w_gate.astype(BF16), ple_w_proj.astype(BF16)
    kvs = None
    for i in range(DEPTH):
        if i < N_A_LAYERS:
            w_in = jnp.pad(a_w_in[i], ((0, 0), (0, A_IN_PAD - A_IN_COLS))).astype(BF16)
            q, k, v, z, gates = _a_projection(xf, seq, attn_norm[i], w_in, a_conv_w[i])
            mix = _gated_deltanet(*(t.reshape(bsz, seq, -1) for t in (q, k, v, z, gates)),
                                  a_log[i], a_dt_bias[i], a_out_norm[i]).reshape(m, A_QK)
            w_o = a_w_out[i]
        else:
            j = i - N_A_LAYERS
            w_q = b_w_q[j].astype(BF16)
            if kvs is None:
                outs = _b_projection(xf, bsz, seq, rope, attn_norm[i], w_q, kv_norm,
                                     _group_kv_columns(b_w_kv).astype(BF16))
                qs, kvs = outs[:B_N_GROUPS], outs[B_N_GROUPS:]
            else:
                qs = _b_projection(xf, bsz, seq, rope, attn_norm[i], w_q)
            mix = _dilated_attention(qs, kvs, biases, seq)
            w_o = b_w_o[j]
        xf = _layer_tail(xf, mix, p, i, w_o.astype(BF16), g_mlp, w_up, w_dn, w_gate, w_ple,
                         g_final, final=(i == DEPTH - 1))
    return xf.reshape(bsz, seq, d)
```
